```python
import math
import jax, jax.numpy as jnp
from jax import lax
import numpy as np


D_MODEL = 1024
BATCH = 4
SEQ = 8192
DEPTH = 4

CHUNK = 64
Q_BLOCK = 128
EPS = 1e-6

D_MIX = D_MODEL
CONV_W = D_MIX // 4
CONV_K = 3
DIFF_HEADS = 4
DIFF_DH = D_MIX // 16
DIFF_DV = 2 * DIFF_DH
DIFF_W = DIFF_HEADS * DIFF_DV
GLA_HEADS = 4
GLA_DK = D_MIX // 16
GLA_DV = D_MIX // 16
GLA_W = GLA_HEADS * GLA_DV
GLA_RANK = 16
GLA_TAU = 16.0
D_FF = 4 * D_MODEL

SIZES = (CONV_W, CONV_W, CONV_W,
         DIFF_HEADS * 2 * DIFF_DH, DIFF_HEADS * 2 * DIFF_DH, DIFF_W,
         GLA_HEADS * GLA_DK, GLA_HEADS * GLA_DK, GLA_W, GLA_W, GLA_RANK)
D_IN = sum(SIZES)
SPLITS = tuple(int(s) for s in np.cumsum(SIZES)[:-1])

kernel_name = 'hybrid_conv_diffattn_gla_block'


def rms_norm(x, g):
    xf = x.astype(jnp.float32)
    y = xf * lax.rsqrt(jnp.mean(jnp.square(xf), axis=-1, keepdims=True) + EPS)
    return (y * g.astype(jnp.float32)).astype(x.dtype)


def short_conv(u, b_gate, c_gate, w):
    z = c_gate * u
    y = lax.conv_general_dilated(z, w.astype(z.dtype)[:, None, :], window_strides=(1,),
                                 padding=[(CONV_K - 1, 0)],
                                 dimension_numbers=('NWC', 'WIO', 'NWC'),
                                 feature_group_count=CONV_W)
    return b_gate * y


def diff_attention(q, k, v, q_g, k_g, lam, sub_g, lam_init):
    bsz, seq = q.shape[0], q.shape[1]
    q = rms_norm(q, q_g).astype(jnp.float32) * (DIFF_DH ** -0.5)
    k = rms_norm(k, k_g).astype(jnp.float32)
    v = v.astype(jnp.float32)
    nb = seq // Q_BLOCK
    qb = jnp.moveaxis(q.reshape(bsz, nb, Q_BLOCK, DIFF_HEADS, 2, DIFF_DH), 1, 0)
    pos = jnp.arange(seq)
    key_chunk = pos // CHUNK
    slopes = jnp.asarray([2.0 ** (-8.0 * (h + 1) / DIFF_HEADS) for h in range(DIFF_HEADS)],
                         dtype=jnp.float32)

    def block(args):
        q_blk, i = args
        tq = i * Q_BLOCK + jnp.arange(Q_BLOCK)
        dist = jnp.abs(tq[:, None] - pos[None, :]).astype(jnp.float32)
        visible = key_chunk[None, :] <= (tq // CHUNK)[:, None]
        bias = jnp.where(visible[None], -slopes[:, None, None] * dist[None], -jnp.inf)
        s = jnp.einsum('bqhcd,bkhcd->bchqk', q_blk, k) + bias
        p = jax.nn.softmax(s, axis=-1)
        a = p[:, 0] - lam * p[:, 1]
        return jnp.einsum('bhqk,bkhe->bqhe', a, v)

    o = lax.map(block, (qb, jnp.arange(nb)))
    o = jnp.moveaxis(o, 0, 1).reshape(bsz, seq, DIFF_HEADS, DIFF_DV)
    return rms_norm(o, sub_g) * (1.0 - lam_init)


def gla(q, k, v, g_out, a_lr, a_w, a_b, norm_g):
    bsz, seq = q.shape[0], q.shape[1]
    nc = seq // CHUNK
    shp_k = (bsz, nc, CHUNK, GLA_HEADS, GLA_DK)
    log_a = jax.nn.log_sigmoid((a_lr @ a_w + a_b).astype(jnp.float32)) / GLA_TAU
    b = jnp.cumsum(log_a.reshape(shp_k), axis=2)
    b_last = b[:, :, -1]
    qf = q.astype(jnp.float32).reshape(shp_k) * (GLA_DK ** -0.5)
    kf = k.astype(jnp.float32).reshape(shp_k)
    vf = v.astype(jnp.float32).reshape(bsz, nc, CHUNK, GLA_HEADS, GLA_DV)
    q_in = qf * jnp.exp(b)
    k_in = kf * jnp.exp(-b)
    k_end = kf * jnp.exp(b_last[:, :, None] - b)
    causal = jnp.tril(jnp.ones((CHUNK, CHUNK), dtype=bool))
    att = jnp.where(causal, jnp.einsum('bcthd,bcshd->bchts', q_in, k_in), 0.0)
    o_intra = jnp.einsum('bchts,bcshe->bcthe', att, vf)
    kv = jnp.einsum('bcshd,bcshe->bchde', k_end, vf)

    def step(state, inp):
        dec, kv_c = inp
        return dec[..., None] * state + kv_c, state

    s0 = jnp.zeros((bsz, GLA_HEADS, GLA_DK, GLA_DV), jnp.float32)
    _, s_prev = lax.scan(step, s0, (jnp.moveaxis(jnp.exp(b_last), 1, 0), jnp.moveaxis(kv, 1, 0)))
    o_inter = jnp.einsum('bcthd,bchde->bcthe', q_in, jnp.moveaxis(s_prev, 0, 1))
    o = (o_intra + o_inter).reshape(bsz, seq, GLA_HEADS, GLA_DV)
    o = rms_norm(o, norm_g).reshape(bsz, seq, GLA_W)
    return o * jax.nn.silu(g_out.astype(jnp.float32))


def setup_inputs(seed: int = 0) -> dict:
    key = jax.random.key(seed)
    ks = jax.random.split(key, 15)

    def nrm(k, shape, scale):
        return jax.random.normal(k, shape, jnp.float32) * scale

    return {
        'x': nrm(ks[0], (BATCH, SEQ, D_MODEL), 1.0),
        'ln1_g': 1.0 + nrm(ks[1], (DEPTH, D_MODEL), 0.02),
        'w_in': nrm(ks[2], (DEPTH, D_MODEL, D_IN), D_MODEL ** -0.5),
        'conv_w': nrm(ks[3], (DEPTH, CONV_K, CONV_W), CONV_K ** -0.5),
        'q_norm_g': 1.0 + nrm(ks[4], (DEPTH, DIFF_DH), 0.02),
        'k_norm_g': 1.0 + nrm(ks[5], (DEPTH, DIFF_DH), 0.02),
        'diff_lambda': nrm(ks[6], (DEPTH, 4, DIFF_DH), 0.1),
        'diff_subln_g': 1.0 + nrm(ks[7], (DEPTH, DIFF_DV), 0.02),
        'gla_alpha_w': nrm(ks[8], (DEPTH, GLA_RANK, GLA_HEADS * GLA_DK), GLA_RANK ** -0.5),
        'gla_alpha_b': nrm(ks[9], (DEPTH, GLA_HEADS * GLA_DK), 0.01),
        'gla_norm_g': 1.0 + nrm(ks[10], (DEPTH, GLA_DV), 0.02),
        'w_out': nrm(ks[11], (DEPTH, D_MIX, D_MODEL), D_MIX ** -0.5),
        'ln2_g': 1.0 + nrm(ks[12], (DEPTH, D_MODEL), 0.02),
        'w_mlp1': nrm(ks[13], (DEPTH, D_MODEL, D_FF), D_MODEL ** -0.5),
        'w_mlp2': nrm(ks[14], (DEPTH, D_FF, D_MODEL), D_FF ** -0.5),
    }


def reference(x, ln1_g, w_in, conv_w, q_norm_g, k_norm_g, diff_lambda, diff_subln_g,
              gla_alpha_w, gla_alpha_b, gla_norm_g, w_out, ln2_g, w_mlp1, w_mlp2):
    bsz, seq = x.shape[0], x.shape[1]
    for l in range(DEPTH):
        h = rms_norm(x, ln1_g[l])
        z = h @ w_in[l]
        (u, c_b, c_c, d_q, d_k, d_v, g_q, g_k, g_v, g_g, g_a) = jnp.split(z, SPLITS, axis=-1)
        y_conv = short_conv(u, c_b, c_c, conv_w[l])
        lam_init = 0.8 - 0.6 * math.exp(-0.3 * l)
        lp = diff_lambda[l].astype(jnp.float32)
        lam = jnp.exp(jnp.sum(lp[0] * lp[1])) - jnp.exp(jnp.sum(lp[2] * lp[3])) + lam_init
        y_diff = diff_attention(d_q.reshape(bsz, seq, DIFF_HEADS, 2, DIFF_DH),
                                d_k.reshape(bsz, seq, DIFF_HEADS, 2, DIFF_DH),
                                d_v.reshape(bsz, seq, DIFF_HEADS, DIFF_DV),
                                q_norm_g[l], k_norm_g[l], lam, diff_subln_g[l], lam_init)
        y_gla = gla(g_q, g_k, g_v, g_g, g_a, gla_alpha_w[l], gla_alpha_b[l], gla_norm_g[l])
        y = jnp.concatenate([y_conv.astype(x.dtype),
                             y_diff.reshape(bsz, seq, DIFF_W).astype(x.dtype),
                             y_gla.astype(x.dtype)], axis=-1)
        x = x + y @ w_out[l]
        h2 = rms_norm(x, ln2_g[l])
        x = x + jnp.square(jax.nn.relu(h2 @ w_mlp1[l])) @ w_mlp2[l]
    return x
```

```python
import functools
import math

import jax
import jax.numpy as jnp
from jax import lax
from jax.experimental import pallas as pl
from jax.experimental.pallas import tpu as pltpu

F32 = jnp.float32
BF16 = jnp.bfloat16

D_MODEL = 1024
CHUNK = 64
EPS = 1e-6
CONV_W = 256
CONV_K = 3
DIFF_HEADS = 4
DIFF_DH = 64
DIFF_DV = 128
DIFF_W = DIFF_HEADS * DIFF_DV
GLA_HEADS = 4
GLA_DK = 64
GLA_DV = 64
GLA_W = GLA_HEADS * GLA_DV
GLA_RANK = 16
GLA_TAU = 16.0
D_FF = 4 * D_MODEL

_C_CONV = 0
_C_Q = 3 * CONV_W
_C_K = _C_Q + DIFF_HEADS * 2 * DIFF_DH
_C_V = _C_K + DIFF_HEADS * 2 * DIFF_DH
_C_G = _C_V + DIFF_W
_C_A = _C_G + 4 * GLA_W
D_MAIN = _C_A

LANES = 128
AUG_W = LANES
HEAD_W = 2 * DIFF_DH + AUG_W
MASKED = -1e30
VMEM_LIMIT_BYTES = 52 * 1024 * 1024


def _nt_dot(a, b):
    return lax.dot_general(a, b, (((1,), (1,)), ((), ())), preferred_element_type=F32)


def _dot(a, b):
    return jnp.dot(a, b, preferred_element_type=F32)


def _in_proj_kernel(x_ref, g1_ref, w_ref, wa_ref, convw_ref, qg_ref, kg_ref, bd_ref,
                    augq_ref, augk_ref,
                    yconv_ref, qa_ref, ka_ref, va_ref, zg_ref, za_ref, carry_ref,
                    *, tiles_per_seq):
    i = pl.program_id(0)
    tm = x_ref.shape[0]
    x = x_ref[...]
    ms = jnp.mean(x * x, axis=-1, keepdims=True)
    h = (x * lax.rsqrt(ms + EPS) * g1_ref[...]).astype(BF16)

    zc = _dot(h, w_ref[:, _C_CONV:_C_Q])
    u, cb, cc = zc[:, :CONV_W], zc[:, CONV_W:2 * CONV_W], zc[:, 2 * CONV_W:]
    z = cc * u

    @pl.when(i % tiles_per_seq == 0)
    def _():
        carry_ref[...] = jnp.zeros_like(carry_ref)

    prev = carry_ref[...]
    row = lax.broadcasted_iota(jnp.int32, z.shape, 0)
    z1 = jnp.where(row == 0, prev[7:8, :], pltpu.roll(z, 1, 0))
    z2 = jnp.where(row == 0, prev[6:7, :],
                   jnp.where(row == 1, prev[7:8, :], pltpu.roll(z, 2, 0)))
    cw = convw_ref[...]
    yconv_ref[...] = (cb * (cw[0:1, :] * z2 + cw[1:2, :] * z1 + cw[2:3, :] * z)).astype(BF16)
    carry_ref[...] = z[tm - 8:, :]

    def seg_norm(zz, g_ref):
        ssq = _dot((zz * zz).astype(BF16), bd_ref[...])
        return zz * lax.rsqrt(ssq * (1.0 / DIFF_DH) + EPS) * g_ref[...]

    qn = seg_norm(_dot(h, w_ref[:, _C_Q:_C_K]), qg_ref).astype(BF16)
    kn = seg_norm(_dot(h, w_ref[:, _C_K:_C_V]), kg_ref).astype(BF16)
    zv = _dot(h, w_ref[:, _C_V:_C_G]).astype(BF16)
    lane = lax.broadcasted_iota(jnp.int32, (tm, AUG_W), 1)
    ones_col = jnp.where(lane == 0, 1.0, 0.0).astype(BF16)
    for hd in range(DIFF_HEADS):
        src = slice(hd * 2 * DIFF_DH, (hd + 1) * 2 * DIFF_DH)
        aug = slice(hd * AUG_W, (hd + 1) * AUG_W)
        lo, mid, hi = hd * HEAD_W, hd * HEAD_W + 2 * DIFF_DH, (hd + 1) * HEAD_W
        qa_ref[:, lo:mid] = qn[:, src]
        qa_ref[:, mid:hi] = augq_ref[:, aug]
        ka_ref[:, lo:mid] = kn[:, src]
        ka_ref[:, mid:hi] = augk_ref[:, aug]
        va_ref[:, lo:mid] = zv[:, src]
        va_ref[:, mid:hi] = ones_col

    zg_ref[...] = _dot(h, w_ref[:, _C_G:_C_A]).astype(BF16)
    za_ref[...] = _dot(h, wa_ref[...]).astype(BF16)


def _in_proj(x, g1, w, wa, convw, qg, kg, bd, augq, augk, *, seq, tm):
    t = x.shape[0]
    tps = seq // tm
    row = lambda i: (i, 0)
    fixed = lambda i: (0, 0)
    pos = lambda i: (i % tps, 0)
    out_w = DIFF_HEADS * HEAD_W
    return pl.pallas_call(
        functools.partial(_in_proj_kernel, tiles_per_seq=tps),
        grid=(t // tm,),
        in_specs=[
            pl.BlockSpec((tm, D_MODEL), row),
            pl.BlockSpec((1, D_MODEL), fixed),
            pl.BlockSpec((D_MODEL, D_MAIN), fixed),
            pl.BlockSpec((D_MODEL, LANES), fixed),
            pl.BlockSpec((8, CONV_W), fixed),
            pl.BlockSpec((1, DIFF_W), fixed),
            pl.BlockSpec((1, DIFF_W), fixed),
            pl.BlockSpec((DIFF_W, DIFF_W), fixed),
            pl.BlockSpec((tm, DIFF_HEADS * AUG_W), pos),
            pl.BlockSpec((tm, DIFF_HEADS * AUG_W), pos),
        ],
        out_specs=[
            pl.BlockSpec((tm, CONV_W), row),
            pl.BlockSpec((tm, out_w), row),
            pl.BlockSpec((tm, out_w), row),
            pl.BlockSpec((tm, out_w), row),
            pl.BlockSpec((tm, 4 * GLA_W), row),
            pl.BlockSpec((tm, LANES), row),
        ],
        out_shape=[
            jax.ShapeDtypeStruct((t, CONV_W), BF16),
            jax.ShapeDtypeStruct((t, out_w), BF16),
            jax.ShapeDtypeStruct((t, out_w), BF16),
            jax.ShapeDtypeStruct((t, out_w), BF16),
            jax.ShapeDtypeStruct((t, 4 * GLA_W), BF16),
            jax.ShapeDtypeStruct((t, LANES), BF16),
        ],
        scratch_shapes=[pltpu.VMEM((8, CONV_W), F32)],
        compiler_params=pltpu.CompilerParams(
            dimension_semantics=("arbitrary",), vmem_limit_bytes=VMEM_LIMIT_BYTES),
        name="in_proj",
    )(x, g1, w, wa, convw, qg, kg, bd, augq, augk)


def _diff_attn_kernel(scal_ref, q_ref, k_ref, v_ref, subg_ref, o_ref, acc0_ref, acc1_ref):
    hd = pl.program_id(1)
    qi = pl.program_id(2)
    tq = q_ref.shape[0]
    lam = scal_ref[0]
    out_scale = scal_ref[1]
    slope = scal_ref[2 + hd]

    qa = q_ref[...]
    lane = lax.broadcasted_iota(jnp.int32, qa.shape, 1)
    zero = jnp.zeros_like(qa)
    q0 = jnp.where((lane >= DIFF_DH) & (lane < 2 * DIFF_DH), zero, qa)
    q1 = jnp.where(lane < DIFF_DH, zero, qa)
    acc0_ref[...] = jnp.zeros_like(acc0_ref)
    acc1_ref[...] = jnp.zeros_like(acc1_ref)

    def accumulate(off, corr):
        kt = k_ref[pl.ds(off, tq), :]
        vt = v_ref[pl.ds(off, tq), :]
        for qc, acc in ((q0, acc0_ref), (q1, acc1_ref)):
            s = _nt_dot(qc, kt)
            if corr is not None:
                s = s + corr
            acc[...] += _dot(jnp.exp(s).astype(BF16), vt)

    def body(j, carry):
        accumulate(pl.multiple_of(j * tq, tq), None)
        return carry

    lax.fori_loop(0, qi, body, 0)

    row = lax.broadcasted_iota(jnp.int32, (tq, tq), 0)
    col = lax.broadcasted_iota(jnp.int32, (tq, tq), 1)
    ahead = col - row
    visible = (col // CHUNK) <= (row // CHUNK)
    corr = jnp.where(ahead <= 0, 0.0,
                     jnp.where(visible, (-2.0 * slope) * ahead.astype(F32), MASKED))
    accumulate(pl.multiple_of(qi * tq, tq), corr)

    a0 = acc0_ref[...]
    a1 = acc1_ref[...]
    o = (a0[:, :DIFF_DV] / a0[:, DIFF_DV:DIFF_DV + 1]
         - lam * (a1[:, :DIFF_DV] / a1[:, DIFF_DV:DIFF_DV + 1]))
    ms = jnp.mean(o * o, axis=-1, keepdims=True)
    o_ref[...] = ((o * lax.rsqrt(ms + EPS) * subg_ref[...]) * out_scale).astype(BF16)


def _diff_attn(scal, qa, ka, va, subg, *, batch, seq, tq):
    nq = seq // tq
    return pl.pallas_call(
        _diff_attn_kernel,
        grid=(batch, DIFF_HEADS, nq),
        in_specs=[
            pl.BlockSpec(memory_space=pltpu.SMEM),
            pl.BlockSpec((tq, HEAD_W), lambda b, h, i: (b * nq + i, h)),
            pl.BlockSpec((seq, HEAD_W), lambda b, h, i: (b, h)),
            pl.BlockSpec((seq, HEAD_W), lambda b, h, i: (b, h)),
            pl.BlockSpec((1, DIFF_DV), lambda b, h, i: (0, 0)),
        ],
        out_specs=pl.BlockSpec((tq, DIFF_DV), lambda b, h, i: (b * nq + i, h)),
        out_shape=jax.ShapeDtypeStruct((batch * seq, DIFF_W), BF16),
        scratch_shapes=[pltpu.VMEM((tq, HEAD_W), F32), pltpu.VMEM((tq, HEAD_W), F32)],
        compiler_params=pltpu.CompilerParams(
            dimension_semantics=("arbitrary", "arbitrary", "arbitrary"),
            vmem_limit_bytes=VMEM_LIMIT_BYTES),
        name="diff_attn",
    )(scal, qa, ka, va, subg)


def _gla_kernel(zg_ref, za_ref, aw_ref, ab_ref, ng_ref, lmat_ref, amask_ref, bd_ref,
                o_ref, st_ref):
    r = zg_ref.shape[0]

    @pl.when(pl.program_id(1) == 0)
    def _():
        st_ref[...] = jnp.zeros_like(st_ref)

    gq = zg_ref[:, 0:GLA_W].astype(F32)
    gk = zg_ref[:, GLA_W:2 * GLA_W].astype(F32)
    gv = zg_ref[:, 2 * GLA_W:3 * GLA_W]
    gg = zg_ref[:, 3 * GLA_W:4 * GLA_W].astype(F32)

    pre = _dot(za_ref[...], aw_ref[...]) + ab_ref[...]
    log_a = (jnp.minimum(pre, 0.0) - jnp.log1p(jnp.exp(-jnp.abs(pre)))) * (1.0 / GLA_TAU)
    hi = log_a.astype(BF16)
    lo = (log_a - hi.astype(F32)).astype(BF16)
    cs = _dot(lmat_ref[...], jnp.concatenate([hi, lo], axis=1))
    b = cs[:r, :GLA_W] + cs[:r, GLA_W:]
    b_tot = cs[r:, :GLA_W] + cs[r:, GLA_W:]

    q_in = ((gq * (GLA_DK ** -0.5)) * jnp.exp(b)).astype(BF16)
    k_in = (gk * jnp.exp(-b)).astype(BF16)
    k_end = gk * jnp.exp(b_tot - b)
    dec = jnp.exp(b_tot)

    lane = lax.broadcasted_iota(jnp.int32, (1, GLA_W), 1)
    amask = amask_ref[...]
    o = jnp.zeros((r, GLA_W), F32)
    for hd in range(GLA_HEADS):
        in_head = (lane >= hd * GLA_DK) & (lane < (hd + 1) * GLA_DK)
        att = _nt_dot(jnp.where(in_head, q_in, jnp.zeros_like(q_in)), k_in)
        att = jnp.where(amask > 0.0, att, 0.0).astype(BF16)
        o = jnp.where(in_head, _dot(att, gv), o)

    gv_t = gv.astype(F32).T.astype(BF16)
    bd = bd_ref[...]
    bd_f = bd.astype(F32)
    rowid = lax.broadcasted_iota(jnp.int32, (r, 1), 0)
    st = st_ref[...]
    inter = []
    for c in range(r // CHUNK):
        rows = slice(c * CHUNK, (c + 1) * CHUNK)
        inter.append(_nt_dot(q_in[rows, :], st.astype(BF16)))
        in_chunk = (rowid >= c * CHUNK) & (rowid < (c + 1) * CHUNK)
        kv_t = _dot(gv_t, jnp.where(in_chunk, k_end, 0.0).astype(BF16))
        st = st * dec[c * CHUNK:c * CHUNK + 1, :] + kv_t * bd_f
    st_ref[...] = st
    o = o + jnp.concatenate(inter, axis=0)

    ssq = _dot((o * o).astype(BF16), bd)
    on = o * lax.rsqrt(ssq * (1.0 / GLA_DV) + EPS) * ng_ref[...]
    o_ref[...] = (on * (gg * (1.0 / (1.0 + jnp.exp(-gg))))).astype(BF16)


def _gla(zg, za, aw, ab, ng, lmat, amask, bd, *, batch, seq, r):
    nt = seq // r
    row = lambda b, t: (b * nt + t, 0)
    fixed = lambda b, t: (0, 0)
    return pl.pallas_call(
        _gla_kernel,
        grid=(batch, nt),
        in_specs=[
            pl.BlockSpec((r, 4 * GLA_W), row),
            pl.BlockSpec((r, LANES), row),
            pl.BlockSpec((LANES, GLA_W), fixed),
            pl.BlockSpec((1, GLA_W), fixed),
            pl.BlockSpec((1, GLA_W), fixed),
            pl.BlockSpec((2 * r, r), fixed),
            pl.BlockSpec((r, r), fixed),
            pl.BlockSpec((GLA_W, GLA_W), fixed),
        ],
        out_specs=pl.BlockSpec((r, GLA_W), row),
        out_shape=jax.ShapeDtypeStruct((batch * seq, GLA_W), BF16),
        scratch_shapes=[pltpu.VMEM((GLA_W, GLA_W), F32)],
        compiler_params=pltpu.CompilerParams(
            dimension_semantics=("arbitrary", "arbitrary"), vmem_limit_bytes=VMEM_LIMIT_BYTES),
        name="gla",
    )(zg, za, aw, ab, ng, lmat, amask, bd)


def _out_mlp_kernel(x_ref, yc_ref, yd_ref, yg_ref, wo_ref, g2_ref, w1_ref, w2_ref,
                    o_ref, h2_ref):
    f = pl.program_id(1)

    @pl.when(f == 0)
    def _():
        y = (_dot(yc_ref[...], wo_ref[0:CONV_W, :])
             + _dot(yd_ref[...], wo_ref[CONV_W:CONV_W + DIFF_W, :])
             + _dot(yg_ref[...], wo_ref[CONV_W + DIFF_W:, :]))
        x1 = x_ref[...] + y
        ms = jnp.mean(x1 * x1, axis=-1, keepdims=True)
        h2_ref[...] = (x1 * lax.rsqrt(ms + EPS) * g2_ref[...]).astype(BF16)
        o_ref[...] = x1

    a = jnp.maximum(_dot(h2_ref[...], w1_ref[...]), 0.0)
    o_ref[...] += _dot((a * a).astype(BF16), w2_ref[...])


def _out_mlp(x, yc, yd, yg, wo, g2, w1, w2, *, tm, tf):
    t = x.shape[0]
    row = lambda i, f: (i, 0)
    fixed = lambda i, f: (0, 0)
    return pl.pallas_call(
        _out_mlp_kernel,
        grid=(t // tm, D_FF // tf),
        in_specs=[
            pl.BlockSpec((tm, D_MODEL), row),
            pl.BlockSpec((tm, CONV_W), row),
            pl.BlockSpec((tm, DIFF_W), row),
            pl.BlockSpec((tm, GLA_W), row),
            pl.BlockSpec((D_MODEL, D_MODEL), fixed),
            pl.BlockSpec((1, D_MODEL), fixed),
            pl.BlockSpec((D_MODEL, tf), lambda i, f: (0, f)),
            pl.BlockSpec((tf, D_MODEL), lambda i, f: (f, 0)),
        ],
        out_specs=pl.BlockSpec((tm, D_MODEL), row),
        out_shape=jax.ShapeDtypeStruct((t, D_MODEL), F32),
        scratch_shapes=[pltpu.VMEM((tm, D_MODEL), BF16)],
        compiler_params=pltpu.CompilerParams(
            dimension_semantics=("arbitrary", "arbitrary"), vmem_limit_bytes=VMEM_LIMIT_BYTES),
        name="out_mlp",
    )(x, yc, yd, yg, wo, g2, w1, w2)


def _tile_sizes(batch, seq):
    return dict(tm_in=min(512, seq), tq=min(512, seq), r_gla=min(256, seq),
                tm_mlp=min(1024, seq), tf=512)


def _block_diag_ones(n, blk):
    idx = jnp.arange(n) // blk
    return idx[:, None] == idx[None, :]


def _position_columns(seq, shift):
    pos = jnp.arange(seq)
    pb = (pos // LANES).astype(F32) * LANES
    pr = (pos % LANES).astype(F32)
    one = jnp.ones((seq,), F32)
    zeros = jnp.zeros((seq, AUG_W - 5), F32)
    augq, augk = [], []
    for hd in range(DIFF_HEADS):
        slope = 2.0 ** (-8.0 * (hd + 1) / DIFF_HEADS)
        augq.append(jnp.concatenate(
            [jnp.stack([one, -slope * pb, -slope * pr, one, -shift * one], axis=1), zeros], axis=1))
        augk.append(jnp.concatenate(
            [jnp.stack([slope * pb, one, one, slope * pr, one], axis=1), zeros], axis=1))
    return (jnp.concatenate(augq, axis=1).astype(BF16), jnp.concatenate(augk, axis=1).astype(BF16))


def kernel(x, ln1_g, w_in, conv_w, q_norm_g, k_norm_g, diff_lambda, diff_subln_g, gla_alpha_w,
           gla_alpha_b, gla_norm_g, w_out, ln2_g, w_mlp1, w_mlp2):
    batch, seq, d_model = x.shape
    depth = w_in.shape[0]
    assert d_model == D_MODEL and seq % (2 * CHUNK) == 0
    ts = _tile_sizes(batch, seq)
    r = ts["r_gla"]

    bd_q = _block_diag_ones(DIFF_W, DIFF_DH).astype(BF16)
    bd_g = _block_diag_ones(GLA_W, GLA_DV).astype(BF16)
    same_chunk = _block_diag_ones(r, CHUNK)
    causal = jnp.arange(r)[:, None] >= jnp.arange(r)[None, :]
    lmat = jnp.concatenate([same_chunk & causal, same_chunk], axis=0).astype(BF16)
    amask = (same_chunk & causal).astype(F32)

    w_main = w_in[:, :, :D_MAIN].astype(BF16)
    w_a = jnp.pad(w_in[:, :, D_MAIN:], ((0, 0), (0, 0), (0, LANES - GLA_RANK))).astype(BF16)
    a_w = jnp.pad(gla_alpha_w, ((0, 0), (0, LANES - GLA_RANK), (0, 0))).astype(BF16)
    conv_w8 = jnp.pad(conv_w, ((0, 0), (0, 8 - CONV_K), (0, 0)))
    w_o = w_out.astype(BF16)
    w_1 = w_mlp1.astype(BF16)
    w_2 = w_mlp2.astype(BF16)

    xf = x.reshape(batch * seq, D_MODEL)
    for l in range(depth):
        lam_init = 0.8 - 0.6 * math.exp(-0.3 * l)
        lp = diff_lambda[l].astype(F32)
        lam = jnp.exp(jnp.sum(lp[0] * lp[1])) - jnp.exp(jnp.sum(lp[2] * lp[3])) + lam_init
        slopes = [2.0 ** (-8.0 * (hd + 1) / DIFF_HEADS) for hd in range(DIFF_HEADS)]
        scal = jnp.stack([lam, jnp.asarray(1.0 - lam_init, F32)]
                         + [jnp.asarray(s, F32) for s in slopes]).astype(F32)
        shift = (DIFF_DH ** 0.5) * jnp.max(jnp.abs(q_norm_g[l])) * jnp.max(jnp.abs(k_norm_g[l]))
        augq, augk = _position_columns(seq, shift)
        qg = jnp.tile(q_norm_g[l] * (DIFF_DH ** -0.5), 2 * DIFF_HEADS)[None, :]
        kg = jnp.tile(k_norm_g[l], 2 * DIFF_HEADS)[None, :]

        yconv, qa, ka, va, zg, za = _in_proj(
            xf, ln1_g[l][None, :], w_main[l], w_a[l], conv_w8[l], qg, kg, bd_q, augq, augk,
            seq=seq, tm=ts["tm_in"])
        ydiff = _diff_attn(scal, qa, ka, va, diff_subln_g[l][None, :],
                           batch=batch, seq=seq, tq=ts["tq"])
        ygla = _gla(zg, za, a_w[l], gla_alpha_b[l][None, :],
                    jnp.tile(gla_norm_g[l], GLA_HEADS)[None, :], lmat, amask, bd_g,
                    batch=batch, seq=seq, r=r)
        xf = _out_mlp(xf, yconv, ydiff, ygla, w_o[l], ln2_g[l][None, :], w_1[l], w_2[l],
                      tm=ts["tm_mlp"], tf=ts["tf"])
    return xf.reshape(batch, seq, D_MODEL)
```

```python
import functools
import math

import jax
import jax.numpy as jnp
from jax import lax
from jax.experimental import pallas as pl
from jax.experimental.pallas import tpu as pltpu

F32 = jnp.float32
BF16 = jnp.bfloat16

D_MODEL = 1024
CHUNK = 64
EPS = 1e-6
CONV_W = 256
CONV_K = 3
DIFF_HEADS = 4
DIFF_DH = 64
DIFF_DV = 128
DIFF_W = DIFF_HEADS * DIFF_DV
GLA_HEADS = 4
GLA_DK = 64
GLA_DV = 64
GLA_W = GLA_HEADS * GLA_DV
GLA_RANK = 16
GLA_TAU = 16.0
D_FF = 4 * D_MODEL

_C_CONV = 0
_C_Q = 3 * CONV_W
_C_K = _C_Q + DIFF_HEADS * 2 * DIFF_DH
_C_V = _C_K + DIFF_HEADS * 2 * DIFF_DH
_C_G = _C_V + DIFF_W
_C_A = _C_G + 4 * GLA_W
D_MAIN = _C_A

LANES = 128
AUG_W = LANES
HEAD_W = 2 * DIFF_DH + AUG_W
SHIFT_COL = 4
V_ROWS = DIFF_DV + 16
MASKED = -1e30
VMEM_LIMIT_BYTES = 52 * 1024 * 1024


def _nt_dot(a, b):
    return lax.dot_general(a, b, (((1,), (1,)), ((), ())), preferred_element_type=F32)


def _dot(a, b):
    return jnp.dot(a, b, preferred_element_type=F32)


def _in_proj_kernel(shift_ref, x_ref, g1_ref, w_ref, wa_ref, convw_ref, qg_ref, kg_ref, bd_ref,
                    augq_ref, augk_ref,
                    yconv_ref, qa_ref, ka_ref, vt_ref, zg_ref, za_ref, carry_ref,
                    *, tiles_per_seq):
    i = pl.program_id(0)
    tm = x_ref.shape[0]
    x = x_ref[...]
    ms = jnp.mean(x * x, axis=-1, keepdims=True)
    h = (x * lax.rsqrt(ms + EPS) * g1_ref[...]).astype(BF16)

    zc = _dot(h, w_ref[:, _C_CONV:_C_Q])
    u, cb, cc = zc[:, :CONV_W], zc[:, CONV_W:2 * CONV_W], zc[:, 2 * CONV_W:]
    z = cc * u

    @pl.when(i % tiles_per_seq == 0)
    def _():
        carry_ref[...] = jnp.zeros_like(carry_ref)

    prev = carry_ref[...]
    row = lax.broadcasted_iota(jnp.int32, z.shape, 0)
    z1 = jnp.where(row == 0, prev[7:8, :], pltpu.roll(z, 1, 0))
    z2 = jnp.where(row == 0, prev[6:7, :],
                   jnp.where(row == 1, prev[7:8, :], pltpu.roll(z, 2, 0)))
    cw = convw_ref[...]
    yconv_ref[...] = (cb * (cw[0:1, :] * z2 + cw[1:2, :] * z1 + cw[2:3, :] * z)).astype(BF16)
    carry_ref[...] = z[tm - 8:, :]

    def seg_norm(zz, g_ref):
        ssq = _dot((zz * zz).astype(BF16), bd_ref[...])
        return zz * lax.rsqrt(ssq * (1.0 / DIFF_DH) + EPS) * g_ref[...]

    qn = seg_norm(_dot(h, w_ref[:, _C_Q:_C_K]), qg_ref).astype(BF16)
    kn = seg_norm(_dot(h, w_ref[:, _C_K:_C_V]), kg_ref).astype(BF16)
    zv = _dot(h, w_ref[:, _C_V:_C_G])
    lane = lax.broadcasted_iota(jnp.int32, (tm, AUG_W), 1)
    sub = lax.broadcasted_iota(jnp.int32, (V_ROWS - DIFF_DV, tm), 0)
    ones_row = jnp.where(sub == 0, 1.0, 0.0).astype(BF16)
    neg_shift = jnp.full((tm, AUG_W), -shift_ref[0], F32)
    for hd in range(DIFF_HEADS):
        src = slice(hd * 2 * DIFF_DH, (hd + 1) * 2 * DIFF_DH)
        aug = slice(hd * AUG_W, (hd + 1) * AUG_W)
        lo, mid, hi = hd * HEAD_W, hd * HEAD_W + 2 * DIFF_DH, (hd + 1) * HEAD_W
        qa_ref[:, lo:mid] = qn[:, src]
        qa_ref[:, mid:hi] = jnp.where(lane == SHIFT_COL, neg_shift,
                                      augq_ref[:, aug].astype(F32)).astype(BF16)
        ka_ref[:, lo:mid] = kn[:, src]
        ka_ref[:, mid:hi] = augk_ref[:, aug]
        vt_ref[hd, 0:DIFF_DV, :] = zv[:, src].T.astype(BF16)
        vt_ref[hd, DIFF_DV:, :] = ones_row

    zg_ref[...] = _dot(h, w_ref[:, _C_G:_C_A]).astype(BF16)
    za_ref[...] = _dot(h, wa_ref[...]).astype(BF16)


def _in_proj(shift, x, g1, w, wa, convw, qg, kg, bd, augq, augk, *, seq, tm):
    t = x.shape[0]
    tps = seq // tm
    batch = t // seq
    row = lambda i: (i, 0)
    fixed = lambda i: (0, 0)
    pos = lambda i: (i % tps, 0)
    out_w = DIFF_HEADS * HEAD_W
    return pl.pallas_call(
        functools.partial(_in_proj_kernel, tiles_per_seq=tps),
        grid=(t // tm,),
        in_specs=[
            pl.BlockSpec(memory_space=pltpu.SMEM),
            pl.BlockSpec((tm, D_MODEL), row),
            pl.BlockSpec((1, D_MODEL), fixed),
            pl.BlockSpec((D_MODEL, D_MAIN), fixed),
            pl.BlockSpec((D_MODEL, LANES), fixed),
            pl.BlockSpec((8, CONV_W), fixed),
            pl.BlockSpec((1, DIFF_W), fixed),
            pl.BlockSpec((1, DIFF_W), fixed),
            pl.BlockSpec((DIFF_W, DIFF_W), fixed),
            pl.BlockSpec((tm, DIFF_HEADS * AUG_W), pos),
            pl.BlockSpec((tm, DIFF_HEADS * AUG_W), pos),
        ],
        out_specs=[
            pl.BlockSpec((tm, CONV_W), row),
            pl.BlockSpec((tm, out_w), row),
            pl.BlockSpec((tm, out_w), row),
            pl.BlockSpec((None, DIFF_HEADS, None, V_ROWS, tm), lambda i: (i // tps, 0, i % tps, 0, 0)),
            pl.BlockSpec((tm, 4 * GLA_W), row),
            pl.BlockSpec((tm, LANES), row),
        ],
        out_shape=[
            jax.ShapeDtypeStruct((t, CONV_W), BF16),
            jax.ShapeDtypeStruct((t, out_w), BF16),
            jax.ShapeDtypeStruct((t, out_w), BF16),
            jax.ShapeDtypeStruct((batch, DIFF_HEADS, tps, V_ROWS, tm), BF16),
            jax.ShapeDtypeStruct((t, 4 * GLA_W), BF16),
            jax.ShapeDtypeStruct((t, LANES), BF16),
        ],
        scratch_shapes=[pltpu.VMEM((8, CONV_W), F32)],
        compiler_params=pltpu.CompilerParams(
            dimension_semantics=("arbitrary",), vmem_limit_bytes=VMEM_LIMIT_BYTES),
        name="in_proj",
    )(shift, x, g1, w, wa, convw, qg, kg, bd, augq, augk)


def _diff_attn_kernel(scal_ref, q_ref, k_ref, vt_ref, subg_ref, o_ref, acc0_ref, acc1_ref, *s_refs):
    hd = pl.program_id(1)
    qi = pl.program_id(2)
    tq = q_ref.shape[0]
    g = vt_ref.shape[-1]
    lam = scal_ref[0]
    out_scale = scal_ref[1]
    slope = scal_ref[2 + hd]

    qt = q_ref[...].astype(F32).T
    feat = lax.broadcasted_iota(jnp.int32, qt.shape, 0)
    qt0 = jnp.where((feat >= DIFF_DH) & (feat < 2 * DIFF_DH), 0.0, qt).astype(BF16)
    qt1 = jnp.where(feat < DIFF_DH, 0.0, qt).astype(BF16)
    acc0_ref[...] = jnp.zeros_like(acc0_ref)
    acc1_ref[...] = jnp.zeros_like(acc1_ref)

    def key_tiles(tiles, corr):
        staged = []
        for t, gi in enumerate(tiles):
            kt = k_ref[pl.ds(pl.multiple_of(gi * g, g), g), :]
            for c, qtc in enumerate((qt0, qt1)):
                s = _dot(kt, qtc)
                if corr is not None:
                    s = s + corr
                s_refs[2 * t + c][...] = s
                staged.append((gi, s_refs[2 * t + c], (acc0_ref, acc1_ref)[c]))
        for gi, s_ref, acc in staged:
            acc[...] += _dot(vt_ref[gi], jnp.exp(s_ref[...]).astype(BF16))

    def key_tile(gi, corr):
        key_tiles((gi,), corr)

    def body(jj, carry):
        key_tiles((2 * jj, 2 * jj + 1), None)
        return carry

    lax.fori_loop(0, qi // 2, body, 0)

    @pl.when(qi % 2 == 1)
    def _():
        key_tile(qi - 1, None)

    key = lax.broadcasted_iota(jnp.int32, (g, tq), 0)
    qry = lax.broadcasted_iota(jnp.int32, (g, tq), 1)
    ahead = key - qry
    visible = (key // CHUNK) <= (qry // CHUNK)
    corr = jnp.where(ahead <= 0, 0.0,
                     jnp.where(visible, (-2.0 * slope) * ahead.astype(F32), MASKED))
    key_tile(qi, corr)

    a0 = acc0_ref[...]
    a1 = acc1_ref[...]
    ot = (a0[:DIFF_DV, :] / a0[DIFF_DV:DIFF_DV + 1, :]
          - lam * (a1[:DIFF_DV, :] / a1[DIFF_DV:DIFF_DV + 1, :]))
    ms = jnp.mean(ot * ot, axis=0, keepdims=True)
    yt = (ot * lax.rsqrt(ms + EPS) * subg_ref[...]) * out_scale
    o_ref[...] = yt.T.astype(BF16)


def _diff_attn(scal, qa, ka, vt, subg, *, batch, seq, tq):
    nq = seq // tq
    return pl.pallas_call(
        _diff_attn_kernel,
        grid=(batch, DIFF_HEADS, nq),
        in_specs=[
            pl.BlockSpec(memory_space=pltpu.SMEM),
            pl.BlockSpec((tq, HEAD_W), lambda b, h, i: (b * nq + i, h)),
            pl.BlockSpec((seq, HEAD_W), lambda b, h, i: (b, h)),
            pl.BlockSpec((None, None, nq, V_ROWS, tq), lambda b, h, i: (b, h, 0, 0, 0)),
            pl.BlockSpec((DIFF_DV, 1), lambda b, h, i: (0, 0)),
        ],
        out_specs=pl.BlockSpec((tq, DIFF_DV), lambda b, h, i: (b * nq + i, h)),
        out_shape=jax.ShapeDtypeStruct((batch * seq, DIFF_W), BF16),
        scratch_shapes=([pltpu.VMEM((V_ROWS, tq), F32)] * 2 + [pltpu.VMEM((tq, tq), F32)] * 4),
        compiler_params=pltpu.CompilerParams(
            dimension_semantics=("arbitrary", "arbitrary", "arbitrary"),
            vmem_limit_bytes=VMEM_LIMIT_BYTES),
        name="diff_attn",
    )(scal, qa, ka, vt, subg)


def _gla_kernel(zg_ref, za_ref, aw_ref, ab_ref, ng_ref, lmat_ref, amask_ref, bd_ref,
                o_ref, st_ref):
    r = zg_ref.shape[0]

    @pl.when(pl.program_id(1) == 0)
    def _():
        st_ref[...] = jnp.zeros_like(st_ref)

    gq = zg_ref[:, 0:GLA_W].astype(F32)
    gk = zg_ref[:, GLA_W:2 * GLA_W].astype(F32)
    gv = zg_ref[:, 2 * GLA_W:3 * GLA_W]
    gg = zg_ref[:, 3 * GLA_W:4 * GLA_W].astype(F32)

    pre = _dot(za_ref[...], aw_ref[...]) + ab_ref[...]
    log_a = (jnp.minimum(pre, 0.0) - jnp.log1p(jnp.exp(-jnp.abs(pre)))) * (1.0 / GLA_TAU)
    hi = log_a.astype(BF16)
    lo = (log_a - hi.astype(F32)).astype(BF16)
    cs = _dot(lmat_ref[...], jnp.concatenate([hi, lo], axis=1))
    b = cs[:r, :GLA_W] + cs[:r, GLA_W:]
    b_tot = cs[r:, :GLA_W] + cs[r:, GLA_W:]

    q_in = ((gq * (GLA_DK ** -0.5)) * jnp.exp(b)).astype(BF16)
    k_in = (gk * jnp.exp(-b)).astype(BF16)
    k_end = gk * jnp.exp(b_tot - b)
    dec = jnp.exp(b_tot)

    lane = lax.broadcasted_iota(jnp.int32, (1, GLA_W), 1)
    amask = amask_ref[...]
    o = jnp.zeros((r, GLA_W), F32)
    for hd in range(GLA_HEADS):
        in_head = (lane >= hd * GLA_DK) & (lane < (hd + 1) * GLA_DK)
        att = _nt_dot(jnp.where(in_head, q_in, jnp.zeros_like(q_in)), k_in)
        att = jnp.where(amask > 0.0, att, 0.0).astype(BF16)
        o = jnp.where(in_head, _dot(att, gv), o)

    gv_t = gv.astype(F32).T.astype(BF16)
    bd = bd_ref[...]
    bd_f = bd.astype(F32)
    rowid = lax.broadcasted_iota(jnp.int32, (r, 1), 0)
    st = st_ref[...]
    inter = []
    for c in range(r // CHUNK):
        rows = slice(c * CHUNK, (c + 1) * CHUNK)
        inter.append(_nt_dot(q_in[rows, :], st.astype(BF16)))
        in_chunk = (rowid >= c * CHUNK) & (rowid < (c + 1) * CHUNK)
        kv_t = _dot(gv_t, jnp.where(in_chunk, k_end, 0.0).astype(BF16))
        st = st * dec[c * CHUNK:c * CHUNK + 1, :] + kv_t * bd_f
    st_ref[...] = st
    o = o + jnp.concatenate(inter, axis=0)

    ssq = _dot((o * o).astype(BF16), bd)
    on = o * lax.rsqrt(ssq * (1.0 / GLA_DV) + EPS) * ng_ref[...]
    o_ref[...] = (on * (gg * (1.0 / (1.0 + jnp.exp(-gg))))).astype(BF16)


def _gla(zg, za, aw, ab, ng, lmat, amask, bd, *, batch, seq, r):
    nt = seq // r
    row = lambda b, t: (b * nt + t, 0)
    fixed = lambda b, t: (0, 0)
    return pl.pallas_call(
        _gla_kernel,
        grid=(batch, nt),
        in_specs=[
            pl.BlockSpec((r, 4 * GLA_W), row),
            pl.BlockSpec((r, LANES), row),
            pl.BlockSpec((LANES, GLA_W), fixed),
            pl.BlockSpec((1, GLA_W), fixed),
            pl.BlockSpec((1, GLA_W), fixed),
            pl.BlockSpec((2 * r, r), fixed),
            pl.BlockSpec((r, r), fixed),
            pl.BlockSpec((GLA_W, GLA_W), fixed),
        ],
        out_specs=pl.BlockSpec((r, GLA_W), row),
        out_shape=jax.ShapeDtypeStruct((batch * seq, GLA_W), BF16),
        scratch_shapes=[pltpu.VMEM((GLA_W, GLA_W), F32)],
        compiler_params=pltpu.CompilerParams(
            dimension_semantics=("arbitrary", "arbitrary"), vmem_limit_bytes=VMEM_LIMIT_BYTES),
        name="gla",
    )(zg, za, aw, ab, ng, lmat, amask, bd)


def _out_mlp_kernel(x_ref, yc_ref, yd_ref, yg_ref, wo_ref, g2_ref, w1_ref, w2_ref,
                    o_ref, h2_ref):
    f = pl.program_id(1)

    @pl.when(f == 0)
    def _():
        y = (_dot(yc_ref[...], wo_ref[0:CONV_W, :])
             + _dot(yd_ref[...], wo_ref[CONV_W:CONV_W + DIFF_W, :])
             + _dot(yg_ref[...], wo_ref[CONV_W + DIFF_W:, :]))
        x1 = x_ref[...] + y
        ms = jnp.mean(x1 * x1, axis=-1, keepdims=True)
        h2_ref[...] = (x1 * lax.rsqrt(ms + EPS) * g2_ref[...]).astype(BF16)
        o_ref[...] = x1

    a = jnp.maximum(_dot(h2_ref[...], w1_ref[...]), 0.0)
    o_ref[...] += _dot((a * a).astype(BF16), w2_ref[...])


def _out_mlp(x, yc, yd, yg, wo, g2, w1, w2, *, tm, tf):
    t = x.shape[0]
    row = lambda i, f: (i, 0)
    fixed = lambda i, f: (0, 0)
    return pl.pallas_call(
        _out_mlp_kernel,
        grid=(t // tm, D_FF // tf),
        in_specs=[
            pl.BlockSpec((tm, D_MODEL), row),
            pl.BlockSpec((tm, CONV_W), row),
            pl.BlockSpec((tm, DIFF_W), row),
            pl.BlockSpec((tm, GLA_W), row),
            pl.BlockSpec((D_MODEL, D_MODEL), fixed),
            pl.BlockSpec((1, D_MODEL), fixed),
            pl.BlockSpec((D_MODEL, tf), lambda i, f: (0, f)),
            pl.BlockSpec((tf, D_MODEL), lambda i, f: (f, 0)),
        ],
        out_specs=pl.BlockSpec((tm, D_MODEL), row),
        out_shape=jax.ShapeDtypeStruct((t, D_MODEL), F32),
        scratch_shapes=[pltpu.VMEM((tm, D_MODEL), BF16)],
        compiler_params=pltpu.CompilerParams(
            dimension_semantics=("arbitrary", "arbitrary"), vmem_limit_bytes=VMEM_LIMIT_BYTES),
        name="out_mlp",
    )(x, yc, yd, yg, wo, g2, w1, w2)


def _tile_sizes(batch, seq):
    return dict(tq=min(512, seq), r_gla=min(256, seq), tm_mlp=min(1024, seq), tf=512)


def _block_diag_ones(n, blk):
    idx = jnp.arange(n) // blk
    return idx[:, None] == idx[None, :]


def _position_columns(seq):
    pos = jnp.arange(seq)
    pb = (pos // LANES).astype(F32) * LANES
    pr = (pos % LANES).astype(F32)
    one = jnp.ones((seq,), F32)
    zero = jnp.zeros((seq,), F32)
    pad = jnp.zeros((seq, AUG_W - 5), F32)
    augq, augk = [], []
    for hd in range(DIFF_HEADS):
        slope = 2.0 ** (-8.0 * (hd + 1) / DIFF_HEADS)
        augq.append(jnp.concatenate(
            [jnp.stack([one, -slope * pb, -slope * pr, one, zero], axis=1), pad], axis=1))
        augk.append(jnp.concatenate(
            [jnp.stack([slope * pb, one, one, slope * pr, one], axis=1), pad], axis=1))
    return (jnp.concatenate(augq, axis=1).astype(BF16), jnp.concatenate(augk, axis=1).astype(BF16))


def kernel(x, ln1_g, w_in, conv_w, q_norm_g, k_norm_g, diff_lambda, diff_subln_g, gla_alpha_w,
           gla_alpha_b, gla_norm_g, w_out, ln2_g, w_mlp1, w_mlp2):
    batch, seq, d_model = x.shape
    depth = w_in.shape[0]
    assert d_model == D_MODEL and seq % (2 * CHUNK) == 0
    ts = _tile_sizes(batch, seq)
    r = ts["r_gla"]

    bd_q = _block_diag_ones(DIFF_W, DIFF_DH).astype(BF16)
    bd_g = _block_diag_ones(GLA_W, GLA_DV).astype(BF16)
    same_chunk = _block_diag_ones(r, CHUNK)
    causal = jnp.arange(r)[:, None] >= jnp.arange(r)[None, :]
    lmat = jnp.concatenate([same_chunk & causal, same_chunk], axis=0).astype(BF16)
    amask = (same_chunk & causal).astype(F32)
    augq, augk = _position_columns(seq)

    w_main = w_in[:, :, :D_MAIN].astype(BF16)
    w_a = jnp.pad(w_in[:, :, D_MAIN:], ((0, 0), (0, 0), (0, LANES - GLA_RANK))).astype(BF16)
    a_w = jnp.pad(gla_alpha_w, ((0, 0), (0, LANES - GLA_RANK), (0, 0))).astype(BF16)
    conv_w8 = jnp.pad(conv_w, ((0, 0), (0, 8 - CONV_K), (0, 0)))
    w_o = w_out.astype(BF16)
    w_1 = w_mlp1.astype(BF16)
    w_2 = w_mlp2.astype(BF16)

    xf = x.reshape(batch * seq, D_MODEL)
    for l in range(depth):
        lam_init = 0.8 - 0.6 * math.exp(-0.3 * l)
        lp = diff_lambda[l].astype(F32)
        lam = jnp.exp(jnp.sum(lp[0] * lp[1])) - jnp.exp(jnp.sum(lp[2] * lp[3])) + lam_init
        slopes = [2.0 ** (-8.0 * (hd + 1) / DIFF_HEADS) for hd in range(DIFF_HEADS)]
        scal = jnp.stack([lam, jnp.asarray(1.0 - lam_init, F32)]
                         + [jnp.asarray(s, F32) for s in slopes]).astype(F32)
        shift = (DIFF_DH ** 0.5) * jnp.max(jnp.abs(q_norm_g[l])) * jnp.max(jnp.abs(k_norm_g[l]))
        qg = jnp.tile(q_norm_g[l] * (DIFF_DH ** -0.5), 2 * DIFF_HEADS)[None, :]
        kg = jnp.tile(k_norm_g[l], 2 * DIFF_HEADS)[None, :]

        yconv, qa, ka, vt, zg, za = _in_proj(
            shift.reshape(1), xf, ln1_g[l][None, :], w_main[l], w_a[l], conv_w8[l], qg, kg, bd_q,
            augq, augk, seq=seq, tm=ts["tq"])
        ydiff = _diff_attn(scal, qa, ka, vt, diff_subln_g[l][:, None],
                           batch=batch, seq=seq, tq=ts["tq"])
        ygla = _gla(zg, za, a_w[l], gla_alpha_b[l][None, :],
                    jnp.tile(gla_norm_g[l], GLA_HEADS)[None, :], lmat, amask, bd_g,
                    batch=batch, seq=seq, r=r)
        xf = _out_mlp(xf, yconv, ydiff, ygla, w_o[l], ln2_g[l][None, :], w_1[l], w_2[l],
                      tm=ts["tm_mlp"], tf=ts["tf"])
    return xf.reshape(batch, seq, D_MODEL)
```

```python
import functools
import math

import jax
import jax.numpy as jnp
import numpy as np
from jax import lax
from jax.experimental import pallas as pl
from jax.experimental.pallas import tpu as pltpu

F32 = jnp.float32
BF16 = jnp.bfloat16

D_MODEL = 1024
CHUNK = 64
EPS = 1e-6
CONV_W = 256
CONV_K = 3
DIFF_HEADS = 4
DIFF_DH = 64
DIFF_DV = 128
DIFF_W = DIFF_HEADS * DIFF_DV
GLA_HEADS = 4
GLA_DK = 64
GLA_DV = 64
GLA_W = GLA_HEADS * GLA_DV
GLA_RANK = 16
GLA_TAU = 16.0
D_FF = 4 * D_MODEL

_C_CONV = 0
_C_Q = 3 * CONV_W
_C_K = _C_Q + DIFF_HEADS * 2 * DIFF_DH
_C_V = _C_K + DIFF_HEADS * 2 * DIFF_DH
_C_G = _C_V + DIFF_W
_C_A = _C_G + 4 * GLA_W
D_MAIN = _C_A

LANES = 128
AUG_W = LANES
HEAD_W = 2 * DIFF_DH + AUG_W
SHIFT_COL = 4
V_ROWS = DIFF_DV + 16
MASKED = -1e30
VMEM_LIMIT_BYTES = 52 * 1024 * 1024


def _nt_dot(a, b):
    return lax.dot_general(a, b, (((1,), (1,)), ((), ())), preferred_element_type=F32)


def _dot(a, b):
    return jnp.dot(a, b, preferred_element_type=F32)


def _in_proj_kernel(shift_ref, x_ref, g1_ref, w_ref, wa_ref, convw_ref, qg_ref, kg_ref, bd_ref,
                    augq_ref, augk_ref,
                    yconv_ref, qa_ref, ka_ref, vt_ref, zg_ref, za_ref, carry_ref,
                    *, tiles_per_seq):
    i = pl.program_id(0)
    tm = x_ref.shape[0]
    x = x_ref[...]
    ms = jnp.mean(x * x, axis=-1, keepdims=True)
    h = (x * lax.rsqrt(ms + EPS) * g1_ref[...]).astype(BF16)

    zc = _dot(h, w_ref[:, _C_CONV:_C_Q])
    u, cb, cc = zc[:, :CONV_W], zc[:, CONV_W:2 * CONV_W], zc[:, 2 * CONV_W:]
    z = cc * u

    @pl.when(i % tiles_per_seq == 0)
    def _():
        carry_ref[...] = jnp.zeros_like(carry_ref)

    prev = carry_ref[...]
    row = lax.broadcasted_iota(jnp.int32, z.shape, 0)
    z1 = jnp.where(row == 0, prev[7:8, :], pltpu.roll(z, 1, 0))
    z2 = jnp.where(row == 0, prev[6:7, :],
                   jnp.where(row == 1, prev[7:8, :], pltpu.roll(z, 2, 0)))
    cw = convw_ref[...]
    yconv_ref[...] = (cb * (cw[0:1, :] * z2 + cw[1:2, :] * z1 + cw[2:3, :] * z)).astype(BF16)
    carry_ref[...] = z[tm - 8:, :]

    def seg_norm(zz, g_ref):
        ssq = _dot((zz * zz).astype(BF16), bd_ref[...])
        return zz * lax.rsqrt(ssq * (1.0 / DIFF_DH) + EPS) * g_ref[...]

    qn = seg_norm(_dot(h, w_ref[:, _C_Q:_C_K]), qg_ref).astype(BF16)
    kn = seg_norm(_dot(h, w_ref[:, _C_K:_C_V]), kg_ref).astype(BF16)
    zv = _dot(h, w_ref[:, _C_V:_C_G])
    lane = lax.broadcasted_iota(jnp.int32, (tm, AUG_W), 1)
    sub = lax.broadcasted_iota(jnp.int32, (V_ROWS - DIFF_DV, tm), 0)
    ones_row = jnp.where(sub == 0, 1.0, 0.0).astype(BF16)
    neg_shift = jnp.full((tm, AUG_W), -shift_ref[0], F32)
    for hd in range(DIFF_HEADS):
        src = slice(hd * 2 * DIFF_DH, (hd + 1) * 2 * DIFF_DH)
        aug = slice(hd * AUG_W, (hd + 1) * AUG_W)
        lo, mid, hi = hd * HEAD_W, hd * HEAD_W + 2 * DIFF_DH, (hd + 1) * HEAD_W
        qa_ref[:, lo:mid] = qn[:, src]
        qa_ref[:, mid:hi] = jnp.where(lane == SHIFT_COL, neg_shift,
                                      augq_ref[:, aug].astype(F32)).astype(BF16)
        ka_ref[:, lo:mid] = kn[:, src]
        ka_ref[:, mid:hi] = augk_ref[:, aug]
        vt_ref[hd, 0:DIFF_DV, :] = zv[:, src].T.astype(BF16)
        vt_ref[hd, DIFF_DV:, :] = ones_row

    za_ref[...] = _dot(h, wa_ref[...]).astype(BF16)
    for part in range(4):
        cols = slice(part * GLA_W, (part + 1) * GLA_W)
        zg_ref[:, cols] = _dot(h, w_ref[:, _C_G + part * GLA_W:_C_G + (part + 1) * GLA_W]).astype(BF16)


def _in_proj(shift, x, g1, w, wa, convw, qg, kg, bd, augq, augk, *, seq, tm):
    t = x.shape[0]
    tps = seq // tm
    batch = t // seq
    row = lambda i: (i, 0)
    fixed = lambda i: (0, 0)
    pos = lambda i: (i % tps, 0)
    out_w = DIFF_HEADS * HEAD_W
    return pl.pallas_call(
        functools.partial(_in_proj_kernel, tiles_per_seq=tps),
        grid=(t // tm,),
        in_specs=[
            pl.BlockSpec(memory_space=pltpu.SMEM),
            pl.BlockSpec((tm, D_MODEL), row),
            pl.BlockSpec((1, D_MODEL), fixed),
            pl.BlockSpec((D_MODEL, D_MAIN), fixed),
            pl.BlockSpec((D_MODEL, LANES), fixed),
            pl.BlockSpec((8, CONV_W), fixed),
            pl.BlockSpec((1, DIFF_W), fixed),
            pl.BlockSpec((1, DIFF_W), fixed),
            pl.BlockSpec((DIFF_W, DIFF_W), fixed),
            pl.BlockSpec((tm, DIFF_HEADS * AUG_W), pos),
            pl.BlockSpec((tm, DIFF_HEADS * AUG_W), pos),
        ],
        out_specs=[
            pl.BlockSpec((tm, CONV_W), row),
            pl.BlockSpec((tm, out_w), row),
            pl.BlockSpec((tm, out_w), row),
            pl.BlockSpec((None, DIFF_HEADS, None, V_ROWS, tm), lambda i: (i // tps, 0, i % tps, 0, 0)),
            pl.BlockSpec((tm, 4 * GLA_W), row),
            pl.BlockSpec((tm, LANES), row),
        ],
        out_shape=[
            jax.ShapeDtypeStruct((t, CONV_W), BF16),
            jax.ShapeDtypeStruct((t, out_w), BF16),
            jax.ShapeDtypeStruct((t, out_w), BF16),
            jax.ShapeDtypeStruct((batch, DIFF_HEADS, tps, V_ROWS, tm), BF16),
            jax.ShapeDtypeStruct((t, 4 * GLA_W), BF16),
            jax.ShapeDtypeStruct((t, LANES), BF16),
        ],
        scratch_shapes=[pltpu.VMEM((8, CONV_W), F32)],
        compiler_params=pltpu.CompilerParams(
            dimension_semantics=("arbitrary",), vmem_limit_bytes=VMEM_LIMIT_BYTES),
        name="in_proj",
    )(shift, x, g1, w, wa, convw, qg, kg, bd, augq, augk)


def _diff_attn_kernel(scal_ref, q_ref, k_ref, vt_ref, subg_ref, o_ref, acc0_ref, acc1_ref, *s_refs):
    hd = pl.program_id(1)
    qi = pl.program_id(2)
    tq = q_ref.shape[0]
    g = vt_ref.shape[-1]
    lam = scal_ref[0]
    out_scale = scal_ref[1]
    slope = scal_ref[2 + hd]

    qt = q_ref[...].astype(F32).T
    feat = lax.broadcasted_iota(jnp.int32, qt.shape, 0)
    qt0 = jnp.where((feat >= DIFF_DH) & (feat < 2 * DIFF_DH), 0.0, qt).astype(BF16)
    qt1 = jnp.where(feat < DIFF_DH, 0.0, qt).astype(BF16)
    acc0_ref[...] = jnp.zeros_like(acc0_ref)
    acc1_ref[...] = jnp.zeros_like(acc1_ref)

    def key_tiles(jobs):
        staged = []
        for t, (gi, q_lo, corr) in enumerate(jobs):
            kt = k_ref[pl.ds(pl.multiple_of(gi * g, g), g), :]
            for c, qtc in enumerate((qt0, qt1)):
                s_ref = s_refs[2 * t + c]
                s = _dot(kt, qtc[:, q_lo:])
                if corr is None:
                    s_ref[:, q_lo:] = s
                else:
                    s_ref[:, q_lo:q_lo + g] = s[:, :g] + corr
                    if q_lo + g < tq:
                        s_ref[:, q_lo + g:] = s[:, g:]
                staged.append((gi, q_lo, s_ref, (acc0_ref, acc1_ref)[c]))
        for gi, q_lo, s_ref, acc in staged:
            acc[:, q_lo:] += _dot(vt_ref[gi], jnp.exp(s_ref[:, q_lo:]).astype(BF16))

    def body(jj, carry):
        key_tiles(((2 * jj, 0, None), (2 * jj + 1, 0, None)))
        return carry

    lax.fori_loop(0, qi, body, 0)

    key = lax.broadcasted_iota(jnp.int32, (g, g), 0)
    qry = lax.broadcasted_iota(jnp.int32, (g, g), 1)
    ahead = key - qry
    visible = (key // CHUNK) <= (qry // CHUNK)
    corr = jnp.where(ahead <= 0, 0.0,
                     jnp.where(visible, (-2.0 * slope) * ahead.astype(F32), MASKED))
    key_tiles(((2 * qi, 0, corr), (2 * qi + 1, g, corr)))

    a0 = acc0_ref[...]
    a1 = acc1_ref[...]
    ot = (a0[:DIFF_DV, :] / a0[DIFF_DV:DIFF_DV + 1, :]
          - lam * (a1[:DIFF_DV, :] / a1[DIFF_DV:DIFF_DV + 1, :]))
    ms = jnp.mean(ot * ot, axis=0, keepdims=True)
    yt = (ot * lax.rsqrt(ms + EPS) * subg_ref[...]) * out_scale
    o_ref[...] = yt.T.astype(BF16)


def _diff_attn(scal, qa, ka, vt, subg, *, batch, seq, tq):
    nq = seq // tq
    g = vt.shape[-1]
    assert tq == 2 * g
    return pl.pallas_call(
        _diff_attn_kernel,
        grid=(batch, DIFF_HEADS, nq),
        in_specs=[
            pl.BlockSpec(memory_space=pltpu.SMEM),
            pl.BlockSpec((tq, HEAD_W), lambda b, h, i: (b * nq + i, h)),
            pl.BlockSpec((seq, HEAD_W), lambda b, h, i: (b, h)),
            pl.BlockSpec((None, None, seq // g, V_ROWS, g), lambda b, h, i: (b, h, 0, 0, 0)),
            pl.BlockSpec((DIFF_DV, 1), lambda b, h, i: (0, 0)),
        ],
        out_specs=pl.BlockSpec((tq, DIFF_DV), lambda b, h, i: (b * nq + i, h)),
        out_shape=jax.ShapeDtypeStruct((batch * seq, DIFF_W), BF16),
        scratch_shapes=([pltpu.VMEM((V_ROWS, tq), F32)] * 2 + [pltpu.VMEM((g, tq), F32)] * 4),
        compiler_params=pltpu.CompilerParams(
            dimension_semantics=("arbitrary", "arbitrary", "arbitrary"),
            vmem_limit_bytes=VMEM_LIMIT_BYTES),
        name="diff_attn",
    )(scal, qa, ka, vt, subg)


def _gla_kernel(zg_ref, za_ref, aw_ref, ab_ref, ng_ref, lmat_ref, amask_ref, bd_ref,
                o_ref, st_ref):
    r = zg_ref.shape[0]

    @pl.when(pl.program_id(1) == 0)
    def _():
        st_ref[...] = jnp.zeros_like(st_ref)

    gq = zg_ref[:, 0:GLA_W].astype(F32)
    gk = zg_ref[:, GLA_W:2 * GLA_W].astype(F32)
    gv = zg_ref[:, 2 * GLA_W:3 * GLA_W]
    gg = zg_ref[:, 3 * GLA_W:4 * GLA_W].astype(F32)

    pre = _dot(za_ref[...], aw_ref[...]) + ab_ref[...]
    log_a = (jnp.minimum(pre, 0.0) - jnp.log1p(jnp.exp(-jnp.abs(pre)))) * (1.0 / GLA_TAU)
    hi = log_a.astype(BF16)
    lo = (log_a - hi.astype(F32)).astype(BF16)
    cs = _dot(lmat_ref[...], jnp.concatenate([hi, lo], axis=1))
    b = cs[:r, :GLA_W] + cs[:r, GLA_W:]
    b_tot = cs[r:, :GLA_W] + cs[r:, GLA_W:]

    q_in = ((gq * (GLA_DK ** -0.5)) * jnp.exp(b)).astype(BF16)
    k_in = (gk * jnp.exp(-b)).astype(BF16)
    k_end = gk * jnp.exp(b_tot - b)
    dec = jnp.exp(b_tot)

    lane = lax.broadcasted_iota(jnp.int32, (1, GLA_W), 1)
    amask = amask_ref[...]
    o = jnp.zeros((r, GLA_W), F32)
    for hd in range(GLA_HEADS):
        in_head = (lane >= hd * GLA_DK) & (lane < (hd + 1) * GLA_DK)
        att = _nt_dot(jnp.where(in_head, q_in, jnp.zeros_like(q_in)), k_in)
        att = jnp.where(amask > 0.0, att, 0.0).astype(BF16)
        o = jnp.where(in_head, _dot(att, gv), o)

    gv_t = gv.astype(F32).T.astype(BF16)
    bd = bd_ref[...]
    bd_f = bd.astype(F32)
    rowid = lax.broadcasted_iota(jnp.int32, (r, 1), 0)
    st = st_ref[...]
    inter = []
    for c in range(r // CHUNK):
        rows = slice(c * CHUNK, (c + 1) * CHUNK)
        inter.append(_nt_dot(q_in[rows, :], st.astype(BF16)))
        in_chunk = (rowid >= c * CHUNK) & (rowid < (c + 1) * CHUNK)
        kv_t = _dot(gv_t, jnp.where(in_chunk, k_end, 0.0).astype(BF16))
        st = st * dec[c * CHUNK:c * CHUNK + 1, :] + kv_t * bd_f
    st_ref[...] = st
    o = o + jnp.concatenate(inter, axis=0)

    ssq = _dot((o * o).astype(BF16), bd)
    on = o * lax.rsqrt(ssq * (1.0 / GLA_DV) + EPS) * ng_ref[...]
    o_ref[...] = (on * (gg * (1.0 / (1.0 + jnp.exp(-gg))))).astype(BF16)


def _gla(zg, za, aw, ab, ng, lmat, amask, bd, *, batch, seq, r):
    nt = seq // r
    row = lambda b, t: (b * nt + t, 0)
    fixed = lambda b, t: (0, 0)
    return pl.pallas_call(
        _gla_kernel,
        grid=(batch, nt),
        in_specs=[
            pl.BlockSpec((r, 4 * GLA_W), row),
            pl.BlockSpec((r, LANES), row),
            pl.BlockSpec((LANES, GLA_W), fixed),
            pl.BlockSpec((1, GLA_W), fixed),
            pl.BlockSpec((1, GLA_W), fixed),
            pl.BlockSpec((2 * r, r), fixed),
            pl.BlockSpec((r, r), fixed),
            pl.BlockSpec((GLA_W, GLA_W), fixed),
        ],
        out_specs=pl.BlockSpec((r, GLA_W), row),
        out_shape=jax.ShapeDtypeStruct((batch * seq, GLA_W), BF16),
        scratch_shapes=[pltpu.VMEM((GLA_W, GLA_W), F32)],
        compiler_params=pltpu.CompilerParams(
            dimension_semantics=("arbitrary", "arbitrary"), vmem_limit_bytes=VMEM_LIMIT_BYTES),
        name="gla",
    )(zg, za, aw, ab, ng, lmat, amask, bd)


def _out_mlp_kernel(x_ref, yc_ref, yd_ref, yg_ref, wo_ref, g2_ref, w1_ref, w2_ref,
                    o_ref, h2_ref):
    f = pl.program_id(1)

    @pl.when(f == 0)
    def _():
        y = (_dot(yc_ref[...], wo_ref[0:CONV_W, :])
             + _dot(yd_ref[...], wo_ref[CONV_W:CONV_W + DIFF_W, :])
             + _dot(yg_ref[...], wo_ref[CONV_W + DIFF_W:, :]))
        x1 = x_ref[...] + y
        ms = jnp.mean(x1 * x1, axis=-1, keepdims=True)
        h2_ref[...] = (x1 * lax.rsqrt(ms + EPS) * g2_ref[...]).astype(BF16)
        o_ref[...] = x1

    a = jnp.maximum(_dot(h2_ref[...], w1_ref[...]), 0.0)
    o_ref[...] += _dot((a * a).astype(BF16), w2_ref[...])


def _out_mlp(x, yc, yd, yg, wo, g2, w1, w2, *, tm, tf):
    t = x.shape[0]
    row = lambda i, f: (i, 0)
    fixed = lambda i, f: (0, 0)
    return pl.pallas_call(
        _out_mlp_kernel,
        grid=(t // tm, D_FF // tf),
        in_specs=[
            pl.BlockSpec((tm, D_MODEL), row),
            pl.BlockSpec((tm, CONV_W), row),
            pl.BlockSpec((tm, DIFF_W), row),
            pl.BlockSpec((tm, GLA_W), row),
            pl.BlockSpec((D_MODEL, D_MODEL), fixed),
            pl.BlockSpec((1, D_MODEL), fixed),
            pl.BlockSpec((D_MODEL, tf), lambda i, f: (0, f)),
            pl.BlockSpec((tf, D_MODEL), lambda i, f: (f, 0)),
        ],
        out_specs=pl.BlockSpec((tm, D_MODEL), row),
        out_shape=jax.ShapeDtypeStruct((t, D_MODEL), F32),
        scratch_shapes=[pltpu.VMEM((tm, D_MODEL), BF16)],
        compiler_params=pltpu.CompilerParams(
            dimension_semantics=("arbitrary", "arbitrary"), vmem_limit_bytes=VMEM_LIMIT_BYTES),
        name="out_mlp",
    )(x, yc, yd, yg, wo, g2, w1, w2)


def _tile_sizes(batch, seq):
    tk = min(512, seq // 2)
    return dict(tk=tk, tq=2 * tk, r_gla=min(256, seq), tm_mlp=min(1024, seq), tf=512)


def _block_diag_ones(n, blk):
    idx = jnp.arange(n) // blk
    return idx[:, None] == idx[None, :]


def _position_columns(seq):
    width = DIFF_HEADS * AUG_W
    pos = lax.broadcasted_iota(jnp.int32, (seq, width), 0)
    lane = lax.broadcasted_iota(jnp.int32, (seq, width), 1)
    col = lane % AUG_W
    slope = jnp.asarray(np.repeat(
        np.asarray([2.0 ** (-8.0 * (hd + 1) / DIFF_HEADS) for hd in range(DIFF_HEADS)], np.float32),
        AUG_W))[None, :]
    hi = slope * (pos - pos % LANES).astype(F32)
    lo = slope * (pos % LANES).astype(F32)
    one = jnp.ones((seq, width), F32)
    zero = jnp.zeros((seq, width), F32)
    augq = jnp.select([col == 0, col == 1, col == 2, col == 3], [one, -hi, -lo, one], zero)
    augk = jnp.select([col == 0, col == 1, col == 2, col == 3, col == SHIFT_COL],
                      [hi, one, one, lo, one], zero)
    return augq.astype(BF16), augk.astype(BF16)


def kernel(x, ln1_g, w_in, conv_w, q_norm_g, k_norm_g, diff_lambda, diff_subln_g, gla_alpha_w,
           gla_alpha_b, gla_norm_g, w_out, ln2_g, w_mlp1, w_mlp2):
    batch, seq, d_model = x.shape
    depth = w_in.shape[0]
    assert d_model == D_MODEL and seq % (2 * CHUNK) == 0
    ts = _tile_sizes(batch, seq)
    r = ts["r_gla"]

    bd_q = _block_diag_ones(DIFF_W, DIFF_DH).astype(BF16)
    bd_g = _block_diag_ones(GLA_W, GLA_DV).astype(BF16)
    same_chunk = _block_diag_ones(r, CHUNK)
    causal = jnp.arange(r)[:, None] >= jnp.arange(r)[None, :]
    lmat = jnp.concatenate([same_chunk & causal, same_chunk], axis=0).astype(BF16)
    amask = (same_chunk & causal).astype(F32)
    augq, augk = _position_columns(seq)

    w_main = w_in[:, :, :D_MAIN].astype(BF16)
    w_a = jnp.pad(w_in[:, :, D_MAIN:], ((0, 0), (0, 0), (0, LANES - GLA_RANK))).astype(BF16)
    a_w = jnp.pad(gla_alpha_w, ((0, 0), (0, LANES - GLA_RANK), (0, 0))).astype(BF16)
    conv_w8 = jnp.pad(conv_w, ((0, 0), (0, 8 - CONV_K), (0, 0)))
    w_o = w_out.astype(BF16)
    w_1 = w_mlp1.astype(BF16)
    w_2 = w_mlp2.astype(BF16)

    xf = x.reshape(batch * seq, D_MODEL)
    for l in range(depth):
        lam_init = 0.8 - 0.6 * math.exp(-0.3 * l)
        lp = diff_lambda[l].astype(F32)
        lam = jnp.exp(jnp.sum(lp[0] * lp[1])) - jnp.exp(jnp.sum(lp[2] * lp[3])) + lam_init
        slopes = [2.0 ** (-8.0 * (hd + 1) / DIFF_HEADS) for hd in range(DIFF_HEADS)]
        scal = jnp.stack([lam, jnp.asarray(1.0 - lam_init, F32)]
                         + [jnp.asarray(s, F32) for s in slopes]).astype(F32)
        shift = (DIFF_DH ** 0.5) * jnp.max(jnp.abs(q_norm_g[l])) * jnp.max(jnp.abs(k_norm_g[l]))
        qg = jnp.tile(q_norm_g[l] * (DIFF_DH ** -0.5), 2 * DIFF_HEADS)[None, :]
        kg = jnp.tile(k_norm_g[l], 2 * DIFF_HEADS)[None, :]

        yconv, qa, ka, vt, zg, za = _in_proj(
            shift.reshape(1), xf, ln1_g[l][None, :], w_main[l], w_a[l], conv_w8[l], qg, kg, bd_q,
            augq, augk, seq=seq, tm=ts["tk"])
        ydiff = _diff_attn(scal, qa, ka, vt, diff_subln_g[l][:, None],
                           batch=batch, seq=seq, tq=ts["tq"])
        ygla = _gla(zg, za, a_w[l], gla_alpha_b[l][None, :],
                    jnp.tile(gla_norm_g[l], GLA_HEADS)[None, :], lmat, amask, bd_g,
                    batch=batch, seq=seq, r=r)
        xf = _out_mlp(xf, yconv, ydiff, ygla, w_o[l], ln2_g[l][None, :], w_1[l], w_2[l],
                      tm=ts["tm_mlp"], tf=ts["tf"])
    return xf.reshape(batch, seq, D_MODEL)
```

```python
import functools
import math

import jax
import jax.numpy as jnp
import numpy as np
from jax import lax
from jax.experimental import pallas as pl
from jax.experimental.pallas import tpu as pltpu

F32 = jnp.float32
BF16 = jnp.bfloat16

D_MODEL = 1024
CHUNK = 64
EPS = 1e-6
CONV_W = 256
CONV_K = 3
DIFF_HEADS = 4
DIFF_DH = 64
DIFF_DV = 128
DIFF_W = DIFF_HEADS * DIFF_DV
GLA_HEADS = 4
GLA_DK = 64
GLA_DV = 64
GLA_W = GLA_HEADS * GLA_DV
GLA_RANK = 16
GLA_TAU = 16.0
D_FF = 4 * D_MODEL

_C_CONV = 0
_C_Q = 3 * CONV_W
_C_K = _C_Q + DIFF_HEADS * 2 * DIFF_DH
_C_V = _C_K + DIFF_HEADS * 2 * DIFF_DH
_C_G = _C_V + DIFF_W
_C_A = _C_G + 4 * GLA_W
D_MAIN = _C_A

LANES = 128
AUG_W = LANES
HEAD_W = 2 * DIFF_DH + AUG_W
SHIFT_COL = 4
V_ROWS = DIFF_DV + 16
MASKED = -1e30
VMEM_LIMIT_BYTES = 52 * 1024 * 1024


def _nt_dot(a, b):
    return lax.dot_general(a, b, (((1,), (1,)), ((), ())), preferred_element_type=F32)


def _dot(a, b):
    return jnp.dot(a, b, preferred_element_type=F32)


def _in_proj_kernel(shift_ref, x_ref, g1_ref, w_ref, wa_ref, convw_ref, qg_ref, kg_ref, bd_ref,
                    augq_ref, augk_ref,
                    yconv_ref, qa_ref, ka_ref, vt_ref, zg_ref, za_ref, carry_ref,
                    *, tiles_per_seq):
    i = pl.program_id(0)
    tm = x_ref.shape[0]
    x = x_ref[...]
    ms = jnp.mean(x * x, axis=-1, keepdims=True)
    h = (x * lax.rsqrt(ms + EPS) * g1_ref[...]).astype(BF16)

    zc = _dot(h, w_ref[:, _C_CONV:_C_Q])
    u, cb, cc = zc[:, :CONV_W], zc[:, CONV_W:2 * CONV_W], zc[:, 2 * CONV_W:]
    z = cc * u

    @pl.when(i % tiles_per_seq == 0)
    def _():
        carry_ref[...] = jnp.zeros_like(carry_ref)

    prev = carry_ref[...]
    row = lax.broadcasted_iota(jnp.int32, z.shape, 0)
    z1 = jnp.where(row == 0, prev[7:8, :], pltpu.roll(z, 1, 0))
    z2 = jnp.where(row == 0, prev[6:7, :],
                   jnp.where(row == 1, prev[7:8, :], pltpu.roll(z, 2, 0)))
    cw = convw_ref[...]
    yconv_ref[...] = (cb * (cw[0:1, :] * z2 + cw[1:2, :] * z1 + cw[2:3, :] * z)).astype(BF16)
    carry_ref[...] = z[tm - 8:, :]

    def seg_norm(zz, g_ref):
        ssq = _dot((zz * zz).astype(BF16), bd_ref[...])
        return zz * lax.rsqrt(ssq * (1.0 / DIFF_DH) + EPS) * g_ref[...]

    qn = seg_norm(_dot(h, w_ref[:, _C_Q:_C_K]), qg_ref).astype(BF16)
    kn = seg_norm(_dot(h, w_ref[:, _C_K:_C_V]), kg_ref).astype(BF16)
    zv = _dot(h, w_ref[:, _C_V:_C_G])
    lane = lax.broadcasted_iota(jnp.int32, (tm, AUG_W), 1)
    sub = lax.broadcasted_iota(jnp.int32, (V_ROWS - DIFF_DV, tm), 0)
    ones_row = jnp.where(sub == 0, 1.0, 0.0).astype(BF16)
    neg_shift = jnp.full((tm, AUG_W), -shift_ref[0], F32)
    for hd in range(DIFF_HEADS):
        src = slice(hd * 2 * DIFF_DH, (hd + 1) * 2 * DIFF_DH)
        aug = slice(hd * AUG_W, (hd + 1) * AUG_W)
        lo, mid, hi = hd * HEAD_W, hd * HEAD_W + 2 * DIFF_DH, (hd + 1) * HEAD_W
        qa_ref[:, lo:mid] = qn[:, src]
        qa_ref[:, mid:hi] = jnp.where(lane == SHIFT_COL, neg_shift,
                                      augq_ref[:, aug].astype(F32)).astype(BF16)
        ka_ref[:, lo:mid] = kn[:, src]
        ka_ref[:, mid:hi] = augk_ref[:, aug]
        vt_ref[hd, 0:DIFF_DV, :] = zv[:, src].T.astype(BF16)
        vt_ref[hd, DIFF_DV:, :] = ones_row

    za_ref[...] = _dot(h, wa_ref[...]).astype(BF16)
    for part in range(4):
        cols = slice(part * GLA_W, (part + 1) * GLA_W)
        zg_ref[:, cols] = _dot(h, w_ref[:, _C_G + part * GLA_W:_C_G + (part + 1) * GLA_W]).astype(BF16)


def _in_proj(shift, x, g1, w, wa, convw, qg, kg, bd, augq, augk, *, seq, tm, layer):
    t = x.shape[0]
    tps = seq // tm
    batch = t // seq
    row = lambda i: (i, 0)
    fixed = lambda i: (0, 0)
    pos = lambda i: (i % tps, 0)
    out_w = DIFF_HEADS * HEAD_W
    return pl.pallas_call(
        functools.partial(_in_proj_kernel, tiles_per_seq=tps),
        grid=(t // tm,),
        in_specs=[
            pl.BlockSpec(memory_space=pltpu.SMEM),
            pl.BlockSpec((tm, D_MODEL), row),
            pl.BlockSpec((1, D_MODEL), fixed),
            pl.BlockSpec((None, D_MODEL, D_MAIN), lambda i: (layer, 0, 0)),
            pl.BlockSpec((D_MODEL, LANES), fixed),
            pl.BlockSpec((8, CONV_W), fixed),
            pl.BlockSpec((1, DIFF_W), fixed),
            pl.BlockSpec((1, DIFF_W), fixed),
            pl.BlockSpec((DIFF_W, DIFF_W), fixed),
            pl.BlockSpec((tm, DIFF_HEADS * AUG_W), pos),
            pl.BlockSpec((tm, DIFF_HEADS * AUG_W), pos),
        ],
        out_specs=[
            pl.BlockSpec((tm, CONV_W), row),
            pl.BlockSpec((tm, out_w), row),
            pl.BlockSpec((tm, out_w), row),
            pl.BlockSpec((None, DIFF_HEADS, None, V_ROWS, tm), lambda i: (i // tps, 0, i % tps, 0, 0)),
            pl.BlockSpec((tm, 4 * GLA_W), row),
            pl.BlockSpec((tm, LANES), row),
        ],
        out_shape=[
            jax.ShapeDtypeStruct((t, CONV_W), BF16),
            jax.ShapeDtypeStruct((t, out_w), BF16),
            jax.ShapeDtypeStruct((t, out_w), BF16),
            jax.ShapeDtypeStruct((batch, DIFF_HEADS, tps, V_ROWS, tm), BF16),
            jax.ShapeDtypeStruct((t, 4 * GLA_W), BF16),
            jax.ShapeDtypeStruct((t, LANES), BF16),
        ],
        scratch_shapes=[pltpu.VMEM((8, CONV_W), F32)],
        compiler_params=pltpu.CompilerParams(
            dimension_semantics=("arbitrary",), vmem_limit_bytes=VMEM_LIMIT_BYTES),
        name="in_proj",
    )(shift, x, g1, w, wa, convw, qg, kg, bd, augq, augk)


def _diff_attn_kernel(scal_ref, q_ref, k_ref, vt_ref, subg_ref, o_ref, acc0_ref, acc1_ref, *s_refs):
    hd = pl.program_id(1)
    qi = pl.program_id(2)
    tq = q_ref.shape[0]
    g = vt_ref.shape[-1]
    lam = scal_ref[0]
    out_scale = scal_ref[1]
    slope = scal_ref[2 + hd]

    qt = q_ref[...].astype(F32).T
    feat = lax.broadcasted_iota(jnp.int32, qt.shape, 0)
    qt0 = jnp.where((feat >= DIFF_DH) & (feat < 2 * DIFF_DH), 0.0, qt).astype(BF16)
    qt1 = jnp.where(feat < DIFF_DH, 0.0, qt).astype(BF16)
    acc0_ref[...] = jnp.zeros_like(acc0_ref)
    acc1_ref[...] = jnp.zeros_like(acc1_ref)

    def key_tiles(jobs):
        staged = []
        for t, (gi, q_lo, corr) in enumerate(jobs):
            kt = k_ref[pl.ds(pl.multiple_of(gi * g, g), g), :]
            for c, qtc in enumerate((qt0, qt1)):
                s_ref = s_refs[2 * t + c]
                s = _dot(kt, qtc[:, q_lo:])
                if corr is None:
                    s_ref[:, q_lo:] = s
                else:
                    s_ref[:, q_lo:q_lo + g] = s[:, :g] + corr
                    if q_lo + g < tq:
                        s_ref[:, q_lo + g:] = s[:, g:]
                staged.append((gi, q_lo, s_ref, (acc0_ref, acc1_ref)[c]))
        for gi, q_lo, s_ref, acc in staged:
            acc[:, q_lo:] += _dot(vt_ref[gi], jnp.exp(s_ref[:, q_lo:]).astype(BF16))

    def body(jj, carry):
        key_tiles(((2 * jj, 0, None), (2 * jj + 1, 0, None)))
        return carry

    nsub = tq // g
    lax.fori_loop(0, qi * (nsub // 2), body, 0)

    key = lax.broadcasted_iota(jnp.int32, (g, g), 0)
    qry = lax.broadcasted_iota(jnp.int32, (g, g), 1)
    ahead = key - qry
    visible = (key // CHUNK) <= (qry // CHUNK)
    corr = jnp.where(ahead <= 0, 0.0,
                     jnp.where(visible, (-2.0 * slope) * ahead.astype(F32), MASKED))
    for a in range(0, nsub, 2):
        key_tiles(((nsub * qi + a, a * g, corr), (nsub * qi + a + 1, (a + 1) * g, corr)))

    a0 = acc0_ref[...]
    a1 = acc1_ref[...]
    ot = (a0[:DIFF_DV, :] / a0[DIFF_DV:DIFF_DV + 1, :]
          - lam * (a1[:DIFF_DV, :] / a1[DIFF_DV:DIFF_DV + 1, :]))
    ms = jnp.mean(ot * ot, axis=0, keepdims=True)
    yt = (ot * lax.rsqrt(ms + EPS) * subg_ref[...]) * out_scale
    o_ref[...] = yt.T.astype(BF16)


def _diff_attn(scal, qa, ka, vt, subg, *, batch, seq, tq):
    nq = seq // tq
    g = vt.shape[-1]
    assert tq % (2 * g) == 0
    return pl.pallas_call(
        _diff_attn_kernel,
        grid=(batch, DIFF_HEADS, nq),
        in_specs=[
            pl.BlockSpec(memory_space=pltpu.SMEM),
            pl.BlockSpec((tq, HEAD_W), lambda b, h, i: (b * nq + i, h)),
            pl.BlockSpec((seq, HEAD_W), lambda b, h, i: (b, h)),
            pl.BlockSpec((None, None, seq // g, V_ROWS, g), lambda b, h, i: (b, h, 0, 0, 0)),
            pl.BlockSpec((DIFF_DV, 1), lambda b, h, i: (0, 0)),
        ],
        out_specs=pl.BlockSpec((tq, DIFF_DV), lambda b, h, i: (b * nq + i, h)),
        out_shape=jax.ShapeDtypeStruct((batch * seq, DIFF_W), BF16),
        scratch_shapes=([pltpu.VMEM((V_ROWS, tq), F32)] * 2 + [pltpu.VMEM((g, tq), F32)] * 4),
        compiler_params=pltpu.CompilerParams(
            dimension_semantics=("arbitrary", "arbitrary", "arbitrary"),
            vmem_limit_bytes=VMEM_LIMIT_BYTES),
        name="diff_attn",
    )(scal, qa, ka, vt, subg)


def _gla_kernel(zg_ref, za_ref, aw_ref, ab_ref, ng_ref, lmat_ref, amask_ref, bd_ref,
                o_ref, st_ref):
    @pl.when(pl.program_id(0) == 0)
    def _():
        st_ref[...] = jnp.zeros_like(st_ref)

    nb, r = zg_ref.shape[0], zg_ref.shape[1]
    seqs = range(nb)
    lane = lax.broadcasted_iota(jnp.int32, (1, GLA_W), 1)
    rowid = lax.broadcasted_iota(jnp.int32, (r, 1), 0)
    bd = bd_ref[...]
    bd_f = bd.astype(F32)

    log_a = []
    for bb in seqs:
        pre = _dot(za_ref[bb], aw_ref[...]) + ab_ref[...]
        log_a.append((jnp.minimum(pre, 0.0) - jnp.log1p(jnp.exp(-jnp.abs(pre)))) * (1.0 / GLA_TAU))
    q_in, k_in, k_end, dec = [], [], [], []
    for bb in seqs:
        hi = log_a[bb].astype(BF16)
        lo = (log_a[bb] - hi.astype(F32)).astype(BF16)
        cs = _dot(lmat_ref[...], jnp.concatenate([hi, lo], axis=1))
        b = cs[:r, :GLA_W] + cs[:r, GLA_W:]
        b_tot = cs[r:, :GLA_W] + cs[r:, GLA_W:]
        gq = zg_ref[bb, :, 0:GLA_W].astype(F32)
        gk = zg_ref[bb, :, GLA_W:2 * GLA_W].astype(F32)
        q_in.append(((gq * (GLA_DK ** -0.5)) * jnp.exp(b)).astype(BF16))
        k_in.append((gk * jnp.exp(-b)).astype(BF16))
        k_end.append(gk * jnp.exp(b_tot - b))
        dec.append(jnp.exp(b_tot))

    o = [jnp.zeros((r, GLA_W), F32) for _ in seqs]
    for hd in range(GLA_HEADS):
        in_head = (lane >= hd * GLA_DK) & (lane < (hd + 1) * GLA_DK)
        for bb in seqs:
            att = _nt_dot(jnp.where(in_head, q_in[bb], jnp.zeros_like(q_in[bb])), k_in[bb])
            att = jnp.where(amask_ref[...] > 0.0, att, 0.0).astype(BF16)
            o[bb] = jnp.where(in_head, _dot(att, zg_ref[bb, :, 2 * GLA_W:3 * GLA_W]), o[bb])

    kv_t = []
    for bb in seqs:
        gv_t = zg_ref[bb, :, 2 * GLA_W:3 * GLA_W].astype(F32).T.astype(BF16)
        per_chunk = []
        for c in range(r // CHUNK):
            in_chunk = (rowid >= c * CHUNK) & (rowid < (c + 1) * CHUNK)
            per_chunk.append(_dot(gv_t, jnp.where(in_chunk, k_end[bb], 0.0).astype(BF16)) * bd_f)
        kv_t.append(per_chunk)
    st = [st_ref[bb] for bb in seqs]
    inter = [[] for _ in seqs]
    for c in range(r // CHUNK):
        rows = slice(c * CHUNK, (c + 1) * CHUNK)
        for bb in seqs:
            inter[bb].append(_nt_dot(q_in[bb][rows, :], st[bb].astype(BF16)))
            st[bb] = st[bb] * dec[bb][c * CHUNK:c * CHUNK + 1, :] + kv_t[bb][c]
    for bb in seqs:
        st_ref[bb] = st[bb]

    for bb in seqs:
        ob = o[bb] + jnp.concatenate(inter[bb], axis=0)
        ssq = _dot((ob * ob).astype(BF16), bd)
        on = ob * lax.rsqrt(ssq * (1.0 / GLA_DV) + EPS) * ng_ref[...]
        gg = zg_ref[bb, :, 3 * GLA_W:4 * GLA_W].astype(F32)
        o_ref[bb] = (on * (gg * (1.0 / (1.0 + jnp.exp(-gg))))).astype(BF16)


def _gla(zg, za, aw, ab, ng, lmat, amask, bd, *, batch, seq, r):
    row = lambda t: (0, t, 0)
    fixed = lambda t: (0, 0)
    return pl.pallas_call(
        _gla_kernel,
        grid=(seq // r,),
        in_specs=[
            pl.BlockSpec((batch, r, 4 * GLA_W), row),
            pl.BlockSpec((batch, r, LANES), row),
            pl.BlockSpec((LANES, GLA_W), fixed),
            pl.BlockSpec((1, GLA_W), fixed),
            pl.BlockSpec((1, GLA_W), fixed),
            pl.BlockSpec((2 * r, r), fixed),
            pl.BlockSpec((r, r), fixed),
            pl.BlockSpec((GLA_W, GLA_W), fixed),
        ],
        out_specs=pl.BlockSpec((batch, r, GLA_W), row),
        out_shape=jax.ShapeDtypeStruct((batch, seq, GLA_W), BF16),
        scratch_shapes=[pltpu.VMEM((batch, GLA_W, GLA_W), F32)],
        compiler_params=pltpu.CompilerParams(
            dimension_semantics=("arbitrary",), vmem_limit_bytes=VMEM_LIMIT_BYTES),
        name="gla",
    )(zg, za, aw, ab, ng, lmat, amask, bd)


def _out_mlp_kernel(x_ref, yc_ref, yd_ref, yg_ref, wo_ref, g2_ref, w1_ref, w2_ref,
                    o_ref, h2_ref):
    f = pl.program_id(1)

    @pl.when(f == 0)
    def _():
        y = (_dot(yc_ref[...], wo_ref[0:CONV_W, :])
             + _dot(yd_ref[...], wo_ref[CONV_W:CONV_W + DIFF_W, :])
             + _dot(yg_ref[...], wo_ref[CONV_W + DIFF_W:, :]))
        x1 = x_ref[...] + y
        ms = jnp.mean(x1 * x1, axis=-1, keepdims=True)
        h2_ref[...] = (x1 * lax.rsqrt(ms + EPS) * g2_ref[...]).astype(BF16)
        o_ref[...] = x1

    a = jnp.maximum(_dot(h2_ref[...], w1_ref[...]), 0.0)
    o_ref[...] += _dot((a * a).astype(BF16), w2_ref[...])


def _out_mlp(x, yc, yd, yg, wo, g2, w1, w2, *, tm, tf, layer):
    t = x.shape[0]
    row = lambda i, f: (i, 0)
    fixed = lambda i, f: (0, 0)
    return pl.pallas_call(
        _out_mlp_kernel,
        grid=(t // tm, D_FF // tf),
        in_specs=[
            pl.BlockSpec((tm, D_MODEL), row),
            pl.BlockSpec((tm, CONV_W), row),
            pl.BlockSpec((tm, DIFF_W), row),
            pl.BlockSpec((tm, GLA_W), row),
            pl.BlockSpec((None, D_MODEL, D_MODEL), lambda i, f: (layer, 0, 0)),
            pl.BlockSpec((1, D_MODEL), fixed),
            pl.BlockSpec((None, D_MODEL, tf), lambda i, f: (layer, 0, f)),
            pl.BlockSpec((None, tf, D_MODEL), lambda i, f: (layer, f, 0)),
        ],
        out_specs=pl.BlockSpec((tm, D_MODEL), row),
        out_shape=jax.ShapeDtypeStruct((t, D_MODEL), F32),
        scratch_shapes=[pltpu.VMEM((tm, D_MODEL), BF16)],
        compiler_params=pltpu.CompilerParams(
            dimension_semantics=("arbitrary", "arbitrary"), vmem_limit_bytes=VMEM_LIMIT_BYTES),
        name="out_mlp",
    )(x, yc, yd, yg, wo, g2, w1, w2)


def _tile_sizes(batch, seq):
    tk = min(512, seq // 4)
    return dict(tk=tk, tq=4 * tk, r_gla=min(256, seq), tm_mlp=min(1024, seq), tf=512)


def _block_diag_ones(n, blk):
    idx = jnp.arange(n) // blk
    return idx[:, None] == idx[None, :]


def _position_columns(seq):
    width = DIFF_HEADS * AUG_W
    pos = lax.broadcasted_iota(jnp.int32, (seq, width), 0)
    lane = lax.broadcasted_iota(jnp.int32, (seq, width), 1)
    col = lane % AUG_W
    slope = jnp.asarray(np.repeat(
        np.asarray([2.0 ** (-8.0 * (hd + 1) / DIFF_HEADS) for hd in range(DIFF_HEADS)], np.float32),
        AUG_W))[None, :]
    hi = slope * (pos - pos % LANES).astype(F32)
    lo = slope * (pos % LANES).astype(F32)
    is_one_q = (col == 0) | (col == 3)
    is_one_k = (col == 1) | (col == 2) | (col == SHIFT_COL)
    augq = jnp.where(is_one_q, 1.0, jnp.where(col == 1, -hi, jnp.where(col == 2, -lo, 0.0)))
    augk = jnp.where(is_one_k, 1.0, jnp.where(col == 0, hi, jnp.where(col == 3, lo, 0.0)))
    return augq.astype(BF16), augk.astype(BF16)


def kernel(x, ln1_g, w_in, conv_w, q_norm_g, k_norm_g, diff_lambda, diff_subln_g, gla_alpha_w,
           gla_alpha_b, gla_norm_g, w_out, ln2_g, w_mlp1, w_mlp2):
    batch, seq, d_model = x.shape
    depth = w_in.shape[0]
    assert d_model == D_MODEL and seq % (2 * CHUNK) == 0
    ts = _tile_sizes(batch, seq)
    r = ts["r_gla"]

    bd_q = _block_diag_ones(DIFF_W, DIFF_DH).astype(BF16)
    bd_g = _block_diag_ones(GLA_W, GLA_DV).astype(BF16)
    same_chunk = _block_diag_ones(r, CHUNK)
    causal = jnp.arange(r)[:, None] >= jnp.arange(r)[None, :]
    lmat = jnp.concatenate([same_chunk & causal, same_chunk], axis=0).astype(BF16)
    amask = (same_chunk & causal).astype(F32)
    augq, augk = _position_columns(seq)

    w_all = w_in.astype(BF16)
    w_a = jnp.pad(w_in[:, :, D_MAIN:], ((0, 0), (0, 0), (0, LANES - GLA_RANK))).astype(BF16)
    a_w = jnp.pad(gla_alpha_w, ((0, 0), (0, LANES - GLA_RANK), (0, 0))).astype(BF16)
    conv_w8 = jnp.pad(conv_w, ((0, 0), (0, 8 - CONV_K), (0, 0)))
    w_o = w_out.astype(BF16)
    w_1 = w_mlp1.astype(BF16)
    w_2 = w_mlp2.astype(BF16)

    xf = x.reshape(batch * seq, D_MODEL)
    for l in range(depth):
        lam_init = 0.8 - 0.6 * math.exp(-0.3 * l)
        lp = diff_lambda[l].astype(F32)
        lam = jnp.exp(jnp.sum(lp[0] * lp[1])) - jnp.exp(jnp.sum(lp[2] * lp[3])) + lam_init
        slopes = [2.0 ** (-8.0 * (hd + 1) / DIFF_HEADS) for hd in range(DIFF_HEADS)]
        scal = jnp.stack([lam, jnp.asarray(1.0 - lam_init, F32)]
                         + [jnp.asarray(s, F32) for s in slopes]).astype(F32)
        shift = (DIFF_DH ** 0.5) * jnp.max(jnp.abs(q_norm_g[l])) * jnp.max(jnp.abs(k_norm_g[l]))
        qg = jnp.tile(q_norm_g[l] * (DIFF_DH ** -0.5), 2 * DIFF_HEADS)[None, :]
        kg = jnp.tile(k_norm_g[l], 2 * DIFF_HEADS)[None, :]

        yconv, qa, ka, vt, zg, za = _in_proj(
            shift.reshape(1), xf, ln1_g[l][None, :], w_all, w_a[l], conv_w8[l], qg, kg, bd_q,
            augq, augk, seq=seq, tm=ts["tk"], layer=l)
        ydiff = _diff_attn(scal, qa, ka, vt, diff_subln_g[l][:, None],
                           batch=batch, seq=seq, tq=ts["tq"])
        ygla = _gla(zg.reshape(batch, seq, 4 * GLA_W), za.reshape(batch, seq, LANES), a_w[l],
                    gla_alpha_b[l][None, :], jnp.tile(gla_norm_g[l], GLA_HEADS)[None, :],
                    lmat, amask, bd_g, batch=batch, seq=seq, r=r)
        xf = _out_mlp(xf, yconv, ydiff, ygla.reshape(batch * seq, GLA_W), w_o, ln2_g[l][None, :],
                      w_1, w_2, tm=ts["tm_mlp"], tf=ts["tf"], layer=l)
    return xf.reshape(batch, seq, D_MODEL)
```

```python
import functools
import math

import jax
import jax.numpy as jnp
import numpy as np
from jax import lax
from jax.experimental import pallas as pl
from jax.experimental.pallas import tpu as pltpu

F32 = jnp.float32
BF16 = jnp.bfloat16

D_MODEL = 1024
CHUNK = 64
EPS = 1e-6
CONV_W = 256
CONV_K = 3
DIFF_HEADS = 4
DIFF_DH = 64
DIFF_DV = 128
DIFF_W = DIFF_HEADS * DIFF_DV
GLA_HEADS = 4
GLA_DK = 64
GLA_DV = 64
GLA_W = GLA_HEADS * GLA_DV
GLA_RANK = 16
GLA_TAU = 16.0
D_FF = 4 * D_MODEL

_C_CONV = 0
_C_Q = 3 * CONV_W
_C_K = _C_Q + DIFF_HEADS * 2 * DIFF_DH
_C_V = _C_K + DIFF_HEADS * 2 * DIFF_DH
_C_G = _C_V + DIFF_W
_C_A = _C_G + 4 * GLA_W
D_MAIN = _C_A

LANES = 128
AUG_W = LANES
HEAD_W = 2 * DIFF_DH + AUG_W
SHIFT_COL = 4
V_ROWS = DIFF_DV + 16
MASKED = -1e30
VMEM_LIMIT_BYTES = 52 * 1024 * 1024


def _nt_dot(a, b):
    return lax.dot_general(a, b, (((1,), (1,)), ((), ())), preferred_element_type=F32)


def _dot(a, b):
    return jnp.dot(a, b, preferred_element_type=F32)


def _in_proj_kernel(x_ref, g1_ref, w_ref, wa_ref, convw_ref, qg_ref, kg_ref, bd_ref,
                    augq_ref, augk_ref,
                    yconv_ref, qa_ref, ka_ref, vt_ref, zg_ref, za_ref, carry_ref,
                    *, tiles_per_seq):
    i = pl.program_id(0)
    g = vt_ref.shape[-1]
    parts = [slice(p * g, (p + 1) * g) for p in range(x_ref.shape[0] // g)]

    @pl.when(i % tiles_per_seq == 0)
    def _():
        carry_ref[...] = jnp.zeros_like(carry_ref)

    prev = carry_ref[...]
    row = lax.broadcasted_iota(jnp.int32, (g, CONV_W), 0)
    cw = convw_ref[...]
    h = []
    for p, rows in enumerate(parts):
        x = x_ref[rows, :]
        ms = jnp.mean(x * x, axis=-1, keepdims=True)
        h.append((x * lax.rsqrt(ms + EPS) * g1_ref[...]).astype(BF16))
        zc = _dot(h[p], w_ref[:, _C_CONV:_C_Q])
        u, cb, cc = zc[:, :CONV_W], zc[:, CONV_W:2 * CONV_W], zc[:, 2 * CONV_W:]
        z = cc * u
        z1 = jnp.where(row == 0, prev[7:8, :], pltpu.roll(z, 1, 0))
        z2 = jnp.where(row == 0, prev[6:7, :],
                       jnp.where(row == 1, prev[7:8, :], pltpu.roll(z, 2, 0)))
        yconv_ref[rows, :] = (cb * (cw[0:1, :] * z2 + cw[1:2, :] * z1 + cw[2:3, :] * z)).astype(BF16)
        prev = z[g - 8:, :]
    carry_ref[...] = prev

    def seg_norm(zz, g_ref):
        ssq = _dot((zz * zz).astype(BF16), bd_ref[...])
        return zz * lax.rsqrt(ssq * (1.0 / DIFF_DH) + EPS) * g_ref[...]

    sub = lax.broadcasted_iota(jnp.int32, (V_ROWS - DIFF_DV, g), 0)
    ones_row = jnp.where(sub == 0, 1.0, 0.0).astype(BF16)
    for col0, g_ref, dst_ref, aug_ref in ((_C_Q, qg_ref, qa_ref, augq_ref), (_C_K, kg_ref, ka_ref, augk_ref)):
        for p, rows in enumerate(parts):
            zn = seg_norm(_dot(h[p], w_ref[:, col0:col0 + DIFF_W]), g_ref).astype(BF16)
            for hd in range(DIFF_HEADS):
                lo, mid, hi = hd * HEAD_W, hd * HEAD_W + 2 * DIFF_DH, (hd + 1) * HEAD_W
                dst_ref[rows, lo:mid] = zn[:, hd * 2 * DIFF_DH:(hd + 1) * 2 * DIFF_DH]
                dst_ref[rows, mid:hi] = aug_ref[rows, hd * AUG_W:(hd + 1) * AUG_W]
    for p, rows in enumerate(parts):
        zv = _dot(h[p], w_ref[:, _C_V:_C_G])
        for hd in range(DIFF_HEADS):
            vt_ref[hd, p, 0:DIFF_DV, :] = zv[:, hd * DIFF_DV:(hd + 1) * DIFF_DV].T.astype(BF16)
            vt_ref[hd, p, DIFF_DV:, :] = ones_row

    for p, rows in enumerate(parts):
        za_ref[rows, :] = _dot(h[p], wa_ref[...]).astype(BF16)
        zg_ref[rows, :] = _dot(h[p], w_ref[:, _C_G:_C_A]).astype(BF16)


def _in_proj(x, g1, w, wa, convw, qg, kg, bd, augq, augk, *, seq, tm, g, layer):
    t = x.shape[0]
    tps = seq // tm
    batch = t // seq
    row = lambda i: (i, 0)
    fixed = lambda i: (0, 0)
    pos = lambda i: (i % tps, 0)
    out_w = DIFF_HEADS * HEAD_W
    return pl.pallas_call(
        functools.partial(_in_proj_kernel, tiles_per_seq=tps),
        grid=(t // tm,),
        in_specs=[
            pl.BlockSpec((tm, D_MODEL), row),
            pl.BlockSpec((1, D_MODEL), fixed),
            pl.BlockSpec((None, D_MODEL, D_MAIN), lambda i: (layer, 0, 0)),
            pl.BlockSpec((D_MODEL, LANES), fixed),
            pl.BlockSpec((8, CONV_W), fixed),
            pl.BlockSpec((1, DIFF_W), fixed),
            pl.BlockSpec((1, DIFF_W), fixed),
            pl.BlockSpec((DIFF_W, DIFF_W), fixed),
            pl.BlockSpec((tm, DIFF_HEADS * AUG_W), pos),
            pl.BlockSpec((tm, DIFF_HEADS * AUG_W), pos),
        ],
        out_specs=[
            pl.BlockSpec((tm, CONV_W), row),
            pl.BlockSpec((tm, out_w), row),
            pl.BlockSpec((tm, out_w), row),
            pl.BlockSpec((None, DIFF_HEADS, tm // g, V_ROWS, g), lambda i: (i // tps, 0, i % tps, 0, 0)),
            pl.BlockSpec((tm, 4 * GLA_W), row),
            pl.BlockSpec((tm, LANES), row),
        ],
        out_shape=[
            jax.ShapeDtypeStruct((t, CONV_W), BF16),
            jax.ShapeDtypeStruct((t, out_w), BF16),
            jax.ShapeDtypeStruct((t, out_w), BF16),
            jax.ShapeDtypeStruct((batch, DIFF_HEADS, seq // g, V_ROWS, g), BF16),
            jax.ShapeDtypeStruct((t, 4 * GLA_W), BF16),
            jax.ShapeDtypeStruct((t, LANES), BF16),
        ],
        scratch_shapes=[pltpu.VMEM((8, CONV_W), F32)],
        compiler_params=pltpu.CompilerParams(
            dimension_semantics=("arbitrary",), vmem_limit_bytes=VMEM_LIMIT_BYTES),
        name="in_proj",
    )(x, g1, w, wa, convw, qg, kg, bd, augq, augk)


def _diff_attn_kernel(scal_ref, q_ref, k_ref, vt_ref, subg_ref, o_ref, acc0_ref, acc1_ref, *s_refs):
    hd = pl.program_id(1)
    qi = pl.program_id(2)
    tq = q_ref.shape[0]
    g = vt_ref.shape[-1]
    lam = scal_ref[0]
    out_scale = scal_ref[1]
    shift = scal_ref[2]
    slope = scal_ref[3 + hd]

    qt = q_ref[...].astype(F32).T
    feat = lax.broadcasted_iota(jnp.int32, qt.shape, 0)
    qt = jnp.where(feat == 2 * DIFF_DH + SHIFT_COL, -shift, qt)
    qt0 = jnp.where((feat >= DIFF_DH) & (feat < 2 * DIFF_DH), 0.0, qt).astype(BF16)
    qt1 = jnp.where(feat < DIFF_DH, 0.0, qt).astype(BF16)
    acc0_ref[...] = jnp.zeros_like(acc0_ref)
    acc1_ref[...] = jnp.zeros_like(acc1_ref)

    def key_tiles(jobs):
        staged = []
        for t, (gi, q_lo, corr) in enumerate(jobs):
            kt = k_ref[pl.ds(pl.multiple_of(gi * g, g), g), :]
            for c, qtc in enumerate((qt0, qt1)):
                s_ref = s_refs[2 * t + c]
                s = _dot(kt, qtc[:, q_lo:])
                if corr is None:
                    s_ref[:, q_lo:] = s
                else:
                    s_ref[:, q_lo:q_lo + g] = s[:, :g] + corr
                    if q_lo + g < tq:
                        s_ref[:, q_lo + g:] = s[:, g:]
                staged.append((gi, q_lo, s_ref, (acc0_ref, acc1_ref)[c]))
        for gi, q_lo, s_ref, acc in staged:
            acc[:, q_lo:] += _dot(vt_ref[gi], jnp.exp(s_ref[:, q_lo:]).astype(BF16))

    def body(jj, carry):
        key_tiles(((2 * jj, 0, None), (2 * jj + 1, 0, None)))
        return carry

    nsub = tq // g
    lax.fori_loop(0, qi * (nsub // 2), body, 0)

    key = lax.broadcasted_iota(jnp.int32, (g, g), 0)
    qry = lax.broadcasted_iota(jnp.int32, (g, g), 1)
    ahead = key - qry
    visible = (key // CHUNK) <= (qry // CHUNK)
    corr = jnp.where(ahead <= 0, 0.0,
                     jnp.where(visible, (-2.0 * slope) * ahead.astype(F32), MASKED))
    for a in range(0, nsub, 2):
        key_tiles(((nsub * qi + a, a * g, corr), (nsub * qi + a + 1, (a + 1) * g, corr)))

    a0 = acc0_ref[...]
    a1 = acc1_ref[...]
    ot = (a0[:DIFF_DV, :] / a0[DIFF_DV:DIFF_DV + 1, :]
          - lam * (a1[:DIFF_DV, :] / a1[DIFF_DV:DIFF_DV + 1, :]))
    ms = jnp.mean(ot * ot, axis=0, keepdims=True)
    yt = (ot * lax.rsqrt(ms + EPS) * subg_ref[...]) * out_scale
    o_ref[...] = yt.T.astype(BF16)


def _diff_attn(scal, qa, ka, vt, subg, *, batch, seq, tq):
    nq = seq // tq
    g = vt.shape[-1]
    assert tq % (2 * g) == 0
    return pl.pallas_call(
        _diff_attn_kernel,
        grid=(batch, DIFF_HEADS, nq),
        in_specs=[
            pl.BlockSpec(memory_space=pltpu.SMEM),
            pl.BlockSpec((tq, HEAD_W), lambda b, h, i: (b * nq + i, h)),
            pl.BlockSpec((seq, HEAD_W), lambda b, h, i: (b, h)),
            pl.BlockSpec((None, None, seq // g, V_ROWS, g), lambda b, h, i: (b, h, 0, 0, 0)),
            pl.BlockSpec((DIFF_DV, 1), lambda b, h, i: (0, 0)),
        ],
        out_specs=pl.BlockSpec((tq, DIFF_DV), lambda b, h, i: (b * nq + i, h)),
        out_shape=jax.ShapeDtypeStruct((batch * seq, DIFF_W), BF16),
        scratch_shapes=([pltpu.VMEM((V_ROWS, tq), F32)] * 2 + [pltpu.VMEM((g, tq), F32)] * 4),
        compiler_params=pltpu.CompilerParams(
            dimension_semantics=("arbitrary", "arbitrary", "arbitrary"),
            vmem_limit_bytes=VMEM_LIMIT_BYTES),
        name="diff_attn",
    )(scal, qa, ka, vt, subg)


def _gla_kernel(zg_ref, za_ref, aw_ref, ab_ref, ng_ref, lmat_ref, amask_ref, bd_ref,
                o_ref, st_ref):
    @pl.when(pl.program_id(0) == 0)
    def _():
        st_ref[...] = jnp.zeros_like(st_ref)

    nb, r = zg_ref.shape[0], zg_ref.shape[1]
    seqs = range(nb)
    lane = lax.broadcasted_iota(jnp.int32, (1, GLA_W), 1)
    rowid = lax.broadcasted_iota(jnp.int32, (r, 1), 0)
    bd = bd_ref[...]
    bd_f = bd.astype(F32)

    log_a = []
    for bb in seqs:
        pre = _dot(za_ref[bb], aw_ref[...]) + ab_ref[...]
        log_a.append((jnp.minimum(pre, 0.0) - jnp.log1p(jnp.exp(-jnp.abs(pre)))) * (1.0 / GLA_TAU))
    q_in, k_in, k_end, dec = [], [], [], []
    for bb in seqs:
        hi = log_a[bb].astype(BF16)
        lo = (log_a[bb] - hi.astype(F32)).astype(BF16)
        cs = _dot(lmat_ref[...], jnp.concatenate([hi, lo], axis=1))
        b = cs[:r, :GLA_W] + cs[:r, GLA_W:]
        b_tot = cs[r:, :GLA_W] + cs[r:, GLA_W:]
        gq = zg_ref[bb, :, 0:GLA_W].astype(F32)
        gk = zg_ref[bb, :, GLA_W:2 * GLA_W].astype(F32)
        q_in.append(((gq * (GLA_DK ** -0.5)) * jnp.exp(b)).astype(BF16))
        k_in.append((gk * jnp.exp(-b)).astype(BF16))
        k_end.append(gk * jnp.exp(b_tot - b))
        dec.append(jnp.exp(b_tot))

    o = [jnp.zeros((r, GLA_W), F32) for _ in seqs]
    for hd in range(GLA_HEADS):
        in_head = (lane >= hd * GLA_DK) & (lane < (hd + 1) * GLA_DK)
        for bb in seqs:
            att = _nt_dot(jnp.where(in_head, q_in[bb], jnp.zeros_like(q_in[bb])), k_in[bb])
            att = jnp.where(amask_ref[...] > 0.0, att, 0.0).astype(BF16)
            o[bb] = jnp.where(in_head, _dot(att, zg_ref[bb, :, 2 * GLA_W:3 * GLA_W]), o[bb])

    kv_t = []
    for bb in seqs:
        gv_t = zg_ref[bb, :, 2 * GLA_W:3 * GLA_W].astype(F32).T.astype(BF16)
        per_chunk = []
        for c in range(r // CHUNK):
            in_chunk = (rowid >= c * CHUNK) & (rowid < (c + 1) * CHUNK)
            per_chunk.append(_dot(gv_t, jnp.where(in_chunk, k_end[bb], 0.0).astype(BF16)) * bd_f)
        kv_t.append(per_chunk)
    st = [st_ref[bb] for bb in seqs]
    inter = [[] for _ in seqs]
    for c in range(r // CHUNK):
        rows = slice(c * CHUNK, (c + 1) * CHUNK)
        for bb in seqs:
            inter[bb].append(_nt_dot(q_in[bb][rows, :], st[bb].astype(BF16)))
            st[bb] = st[bb] * dec[bb][c * CHUNK:c * CHUNK + 1, :] + kv_t[bb][c]
    for bb in seqs:
        st_ref[bb] = st[bb]

    for bb in seqs:
        ob = o[bb] + jnp.concatenate(inter[bb], axis=0)
        ssq = _dot((ob * ob).astype(BF16), bd)
        on = ob * lax.rsqrt(ssq * (1.0 / GLA_DV) + EPS) * ng_ref[...]
        gg = zg_ref[bb, :, 3 * GLA_W:4 * GLA_W].astype(F32)
        o_ref[bb] = (on * (gg * (1.0 / (1.0 + jnp.exp(-gg))))).astype(BF16)


def _gla(zg, za, aw, ab, ng, lmat, amask, bd, *, batch, seq, r):
    row = lambda t: (0, t, 0)
    fixed = lambda t: (0, 0)
    return pl.pallas_call(
        _gla_kernel,
        grid=(seq // r,),
        in_specs=[
            pl.BlockSpec((batch, r, 4 * GLA_W), row),
            pl.BlockSpec((batch, r, LANES), row),
            pl.BlockSpec((LANES, GLA_W), fixed),
            pl.BlockSpec((1, GLA_W), fixed),
            pl.BlockSpec((1, GLA_W), fixed),
            pl.BlockSpec((2 * r, r), fixed),
            pl.BlockSpec((r, r), fixed),
            pl.BlockSpec((GLA_W, GLA_W), fixed),
        ],
        out_specs=pl.BlockSpec((batch, r, GLA_W), row),
        out_shape=jax.ShapeDtypeStruct((batch, seq, GLA_W), BF16),
        scratch_shapes=[pltpu.VMEM((batch, GLA_W, GLA_W), F32)],
        compiler_params=pltpu.CompilerParams(
            dimension_semantics=("arbitrary",), vmem_limit_bytes=VMEM_LIMIT_BYTES),
        name="gla",
    )(zg, za, aw, ab, ng, lmat, amask, bd)


def _out_mlp_kernel(x_ref, yc_ref, yd_ref, yg_ref, wo_ref, g2_ref, w1_ref, w2_ref,
                    o_ref, h2_ref):
    f = pl.program_id(1)

    @pl.when(f == 0)
    def _():
        sub = x_ref.shape[0] // 4
        for p in range(4):
            rows = slice(p * sub, (p + 1) * sub)
            y = (_dot(yc_ref[rows, :], wo_ref[0:CONV_W, :])
                 + _dot(yd_ref[rows, :], wo_ref[CONV_W:CONV_W + DIFF_W, :])
                 + _dot(yg_ref[rows, :], wo_ref[CONV_W + DIFF_W:, :]))
            x1 = x_ref[rows, :] + y
            ms = jnp.mean(x1 * x1, axis=-1, keepdims=True)
            h2_ref[rows, :] = (x1 * lax.rsqrt(ms + EPS) * g2_ref[...]).astype(BF16)
            o_ref[rows, :] = x1

    a = jnp.maximum(_dot(h2_ref[...], w1_ref[...]), 0.0)
    o_ref[...] += _dot((a * a).astype(BF16), w2_ref[...])


def _out_mlp(x, yc, yd, yg, wo, g2, w1, w2, *, tm, tf, layer):
    t = x.shape[0]
    row = lambda i, f: (i, 0)
    fixed = lambda i, f: (0, 0)
    return pl.pallas_call(
        _out_mlp_kernel,
        grid=(t // tm, D_FF // tf),
        in_specs=[
            pl.BlockSpec((tm, D_MODEL), row),
            pl.BlockSpec((tm, CONV_W), row),
            pl.BlockSpec((tm, DIFF_W), row),
            pl.BlockSpec((tm, GLA_W), row),
            pl.BlockSpec((None, D_MODEL, D_MODEL), lambda i, f: (layer, 0, 0)),
            pl.BlockSpec((1, D_MODEL), fixed),
            pl.BlockSpec((None, D_MODEL, tf), lambda i, f: (layer, 0, f)),
            pl.BlockSpec((None, tf, D_MODEL), lambda i, f: (layer, f, 0)),
        ],
        out_specs=pl.BlockSpec((tm, D_MODEL), row),
        out_shape=jax.ShapeDtypeStruct((t, D_MODEL), F32),
        scratch_shapes=[pltpu.VMEM((tm, D_MODEL), BF16)],
        compiler_params=pltpu.CompilerParams(
            dimension_semantics=("arbitrary", "arbitrary"), vmem_limit_bytes=VMEM_LIMIT_BYTES),
        name="out_mlp",
    )(x, yc, yd, yg, wo, g2, w1, w2)


def _tile_sizes(batch, seq):
    tk = min(512, seq // 4)
    return dict(tk=tk, tm_in=2 * tk, tq=4 * tk, r_gla=min(256, seq), tm_mlp=min(1024, seq), tf=512)


def _block_diag_ones(n, blk):
    idx = jnp.arange(n) // blk
    return idx[:, None] == idx[None, :]


def _position_columns(seq):
    width = DIFF_HEADS * AUG_W
    pos = lax.broadcasted_iota(jnp.int32, (seq, width), 0)
    lane = lax.broadcasted_iota(jnp.int32, (seq, width), 1)
    col = lane % AUG_W
    slope = jnp.asarray(np.repeat(
        np.asarray([2.0 ** (-8.0 * (hd + 1) / DIFF_HEADS) for hd in range(DIFF_HEADS)], np.float32),
        AUG_W))[None, :]
    hi = slope * (pos - pos % LANES).astype(F32)
    lo = slope * (pos % LANES).astype(F32)
    is_one_q = (col == 0) | (col == 3)
    is_one_k = (col == 1) | (col == 2) | (col == SHIFT_COL)
    augq = jnp.where(is_one_q, 1.0, jnp.where(col == 1, -hi, jnp.where(col == 2, -lo, 0.0)))
    augk = jnp.where(is_one_k, 1.0, jnp.where(col == 0, hi, jnp.where(col == 3, lo, 0.0)))
    return augq.astype(BF16), augk.astype(BF16)


def kernel(x, ln1_g, w_in, conv_w, q_norm_g, k_norm_g, diff_lambda, diff_subln_g, gla_alpha_w,
           gla_alpha_b, gla_norm_g, w_out, ln2_g, w_mlp1, w_mlp2):
    batch, seq, d_model = x.shape
    depth = w_in.shape[0]
    assert d_model == D_MODEL and seq % (2 * CHUNK) == 0
    ts = _tile_sizes(batch, seq)
    r = ts["r_gla"]

    bd_q = _block_diag_ones(DIFF_W, DIFF_DH).astype(BF16)
    bd_g = _block_diag_ones(GLA_W, GLA_DV).astype(BF16)
    same_chunk = _block_diag_ones(r, CHUNK)
    causal = jnp.arange(r)[:, None] >= jnp.arange(r)[None, :]
    lmat = jnp.concatenate([same_chunk & causal, same_chunk], axis=0).astype(BF16)
    amask = (same_chunk & causal).astype(F32)
    augq, augk = _position_columns(seq)

    w_all = w_in.astype(BF16)
    w_a = jnp.pad(w_in[:, :, D_MAIN:], ((0, 0), (0, 0), (0, LANES - GLA_RANK))).astype(BF16)
    a_w = jnp.pad(gla_alpha_w, ((0, 0), (0, LANES - GLA_RANK), (0, 0))).astype(BF16)
    conv_w8 = jnp.pad(conv_w, ((0, 0), (0, 8 - CONV_K), (0, 0)))
    w_o = w_out.astype(BF16)
    w_1 = w_mlp1.astype(BF16)
    w_2 = w_mlp2.astype(BF16)

    xf = x.reshape(batch * seq, D_MODEL)
    for l in range(depth):
        lam_init = 0.8 - 0.6 * math.exp(-0.3 * l)
        lp = diff_lambda[l].astype(F32)
        lam = jnp.exp(jnp.sum(lp[0] * lp[1])) - jnp.exp(jnp.sum(lp[2] * lp[3])) + lam_init
        slopes = [2.0 ** (-8.0 * (hd + 1) / DIFF_HEADS) for hd in range(DIFF_HEADS)]
        shift = (DIFF_DH ** 0.5) * jnp.max(jnp.abs(q_norm_g[l])) * jnp.max(jnp.abs(k_norm_g[l]))
        scal = jnp.stack([lam, jnp.asarray(1.0 - lam_init, F32), shift]
                         + [jnp.asarray(s, F32) for s in slopes]).astype(F32)
        qg = jnp.tile(q_norm_g[l] * (DIFF_DH ** -0.5), 2 * DIFF_HEADS)[None, :]
        kg = jnp.tile(k_norm_g[l], 2 * DIFF_HEADS)[None, :]

        yconv, qa, ka, vt, zg, za = _in_proj(
            xf, ln1_g[l][None, :], w_all, w_a[l], conv_w8[l], qg, kg, bd_q,
            augq, augk, seq=seq, tm=ts["tm_in"], g=ts["tk"], layer=l)
        ydiff = _diff_attn(scal, qa, ka, vt, diff_subln_g[l][:, None],
                           batch=batch, seq=seq, tq=ts["tq"])
        ygla = _gla(zg.reshape(batch, seq, 4 * GLA_W), za.reshape(batch, seq, LANES), a_w[l],
                    gla_alpha_b[l][None, :], jnp.tile(gla_norm_g[l], GLA_HEADS)[None, :],
                    lmat, amask, bd_g, batch=batch, seq=seq, r=r)
        xf = _out_mlp(xf, yconv, ydiff, ygla.reshape(batch * seq, GLA_W), w_o, ln2_g[l][None, :],
                      w_1, w_2, tm=ts["tm_mlp"], tf=ts["tf"], layer=l)
    return xf.reshape(batch, seq, D_MODEL)
```

```python
import functools
import math

import jax
import jax.numpy as jnp
import numpy as np
from jax import lax
from jax.experimental import pallas as pl
from jax.experimental.pallas import tpu as pltpu

F32 = jnp.float32
BF16 = jnp.bfloat16

D_MODEL = 1024
CHUNK = 64
EPS = 1e-6
CONV_W = 256
CONV_K = 3
DIFF_HEADS = 4
DIFF_DH = 64
DIFF_DV = 128
DIFF_W = DIFF_HEADS * DIFF_DV
GLA_HEADS = 4
GLA_DK = 64
GLA_DV = 64
GLA_W = GLA_HEADS * GLA_DV
GLA_RANK = 16
GLA_TAU = 16.0
D_FF = 4 * D_MODEL

_C_CONV = 0
_C_Q = 3 * CONV_W
_C_K = _C_Q + DIFF_HEADS * 2 * DIFF_DH
_C_V = _C_K + DIFF_HEADS * 2 * DIFF_DH
_C_G = _C_V + DIFF_W
_C_A = _C_G + 4 * GLA_W
D_MAIN = _C_A

LANES = 128
AUG_W = LANES
HEAD_W = 2 * DIFF_DH + AUG_W
SHIFT_COL = 4
V_ROWS = DIFF_DV + 16
MASKED = -1e30
VMEM_LIMIT_BYTES = 52 * 1024 * 1024


def _nt_dot(a, b):
    return lax.dot_general(a, b, (((1,), (1,)), ((), ())), preferred_element_type=F32)


def _dot(a, b):
    return jnp.dot(a, b, preferred_element_type=F32)


def _in_proj_kernel(x_ref, g1_ref, w_ref, wa_ref, convw_ref, qg_ref, kg_ref, bd_ref,
                    augq_ref, augk_ref,
                    yconv_ref, qa_ref, ka_ref, vt_ref, zg_ref, za_ref, carry_ref,
                    *, tiles_per_seq):
    i = pl.program_id(0)
    g = vt_ref.shape[-1]
    parts = [slice(p * g, (p + 1) * g) for p in range(x_ref.shape[0] // g)]

    @pl.when(i % tiles_per_seq == 0)
    def _():
        carry_ref[...] = jnp.zeros_like(carry_ref)

    prev = carry_ref[...]
    row = lax.broadcasted_iota(jnp.int32, (g, CONV_W), 0)
    cw = convw_ref[...]
    h = []
    for p, rows in enumerate(parts):
        x = x_ref[rows, :]
        ms = jnp.mean(x * x, axis=-1, keepdims=True)
        h.append((x * lax.rsqrt(ms + EPS) * g1_ref[...]).astype(BF16))
        zc = _dot(h[p], w_ref[:, _C_CONV:_C_Q])
        u, cb, cc = zc[:, :CONV_W], zc[:, CONV_W:2 * CONV_W], zc[:, 2 * CONV_W:]
        z = cc * u
        z1 = jnp.where(row == 0, prev[7:8, :], pltpu.roll(z, 1, 0))
        z2 = jnp.where(row == 0, prev[6:7, :],
                       jnp.where(row == 1, prev[7:8, :], pltpu.roll(z, 2, 0)))
        yconv_ref[rows, :] = (cb * (cw[0:1, :] * z2 + cw[1:2, :] * z1 + cw[2:3, :] * z)).astype(BF16)
        prev = z[g - 8:, :]
    carry_ref[...] = prev

    def seg_norm(zz, g_ref):
        ssq = _dot((zz * zz).astype(BF16), bd_ref[...])
        return zz * lax.rsqrt(ssq * (1.0 / DIFF_DH) + EPS) * g_ref[...]

    sub = lax.broadcasted_iota(jnp.int32, (V_ROWS - DIFF_DV, g), 0)
    ones_row = jnp.where(sub == 0, 1.0, 0.0).astype(BF16)
    for col0, g_ref, dst_ref, aug_ref in ((_C_Q, qg_ref, qa_ref, augq_ref), (_C_K, kg_ref, ka_ref, augk_ref)):
        for p, rows in enumerate(parts):
            zn = seg_norm(_dot(h[p], w_ref[:, col0:col0 + DIFF_W]), g_ref).astype(BF16)
            for hd in range(DIFF_HEADS):
                lo, mid, hi = hd * HEAD_W, hd * HEAD_W + 2 * DIFF_DH, (hd + 1) * HEAD_W
                dst_ref[rows, lo:mid] = zn[:, hd * 2 * DIFF_DH:(hd + 1) * 2 * DIFF_DH]
                dst_ref[rows, mid:hi] = aug_ref[rows, hd * AUG_W:(hd + 1) * AUG_W]
    for p, rows in enumerate(parts):
        zv = _dot(h[p], w_ref[:, _C_V:_C_G])
        for hd in range(DIFF_HEADS):
            vt_ref[hd, p, 0:DIFF_DV, :] = zv[:, hd * DIFF_DV:(hd + 1) * DIFF_DV].T.astype(BF16)
            vt_ref[hd, p, DIFF_DV:, :] = ones_row

    for p, rows in enumerate(parts):
        za_ref[rows, :] = _dot(h[p], wa_ref[...]).astype(BF16)
        zg_ref[rows, :] = _dot(h[p], w_ref[:, _C_G:_C_A]).astype(BF16)


def _in_proj(x, g1, w, wa, convw, qg, kg, bd, augq, augk, *, seq, tm, g, layer):
    t = x.shape[0]
    tps = seq // tm
    batch = t // seq
    row = lambda i: (i, 0)
    fixed = lambda i: (0, 0)
    pos = lambda i: (i % tps, 0)
    out_w = DIFF_HEADS * HEAD_W
    return pl.pallas_call(
        functools.partial(_in_proj_kernel, tiles_per_seq=tps),
        grid=(t // tm,),
        in_specs=[
            pl.BlockSpec((tm, D_MODEL), row),
            pl.BlockSpec((1, D_MODEL), fixed),
            pl.BlockSpec((None, D_MODEL, D_MAIN), lambda i: (layer, 0, 0)),
            pl.BlockSpec((D_MODEL, LANES), fixed),
            pl.BlockSpec((8, CONV_W), fixed),
            pl.BlockSpec((1, DIFF_W), fixed),
            pl.BlockSpec((1, DIFF_W), fixed),
            pl.BlockSpec((DIFF_W, DIFF_W), fixed),
            pl.BlockSpec((tm, DIFF_HEADS * AUG_W), pos),
            pl.BlockSpec((tm, DIFF_HEADS * AUG_W), pos),
        ],
        out_specs=[
            pl.BlockSpec((tm, CONV_W), row),
            pl.BlockSpec((tm, out_w), row),
            pl.BlockSpec((tm, out_w), row),
            pl.BlockSpec((None, DIFF_HEADS, tm // g, V_ROWS, g), lambda i: (i // tps, 0, i % tps, 0, 0)),
            pl.BlockSpec((tm, 4 * GLA_W), row),
            pl.BlockSpec((tm, LANES), row),
        ],
        out_shape=[
            jax.ShapeDtypeStruct((t, CONV_W), BF16),
            jax.ShapeDtypeStruct((t, out_w), BF16),
            jax.ShapeDtypeStruct((t, out_w), BF16),
            jax.ShapeDtypeStruct((batch, DIFF_HEADS, seq // g, V_ROWS, g), BF16),
            jax.ShapeDtypeStruct((t, 4 * GLA_W), BF16),
            jax.ShapeDtypeStruct((t, LANES), BF16),
        ],
        scratch_shapes=[pltpu.VMEM((8, CONV_W), F32)],
        compiler_params=pltpu.CompilerParams(
            dimension_semantics=("arbitrary",), vmem_limit_bytes=VMEM_LIMIT_BYTES),
        name="in_proj",
    )(x, g1, w, wa, convw, qg, kg, bd, augq, augk)


def _diff_attn_kernel(scal_ref, q_ref, k_ref, vt_ref, subg_ref, o_ref, acc0_ref, acc1_ref, *s_refs):
    hd = pl.program_id(1)
    qi = pl.program_id(2)
    tq = q_ref.shape[0]
    g = vt_ref.shape[-1]
    lam = scal_ref[0]
    out_scale = scal_ref[1]
    shift = scal_ref[2]
    slope = scal_ref[3 + hd]

    qt = q_ref[...].astype(F32).T
    feat = lax.broadcasted_iota(jnp.int32, qt.shape, 0)
    qt = jnp.where(feat == 2 * DIFF_DH + SHIFT_COL, -shift, qt)
    qt0 = jnp.where((feat >= DIFF_DH) & (feat < 2 * DIFF_DH), 0.0, qt).astype(BF16)
    qt1 = jnp.where(feat < DIFF_DH, 0.0, qt).astype(BF16)

    def key_tiles(jobs, init=False):
        staged = []
        for t, (gi, q_lo, corr) in enumerate(jobs):
            kt = k_ref[pl.ds(pl.multiple_of(gi * g, g), g), :]
            for c, qtc in enumerate((qt0, qt1)):
                s_ref = s_refs[2 * t + c]
                s = _dot(kt, qtc[:, q_lo:])
                if corr is None:
                    s_ref[:, q_lo:] = s
                else:
                    s_ref[:, q_lo:q_lo + g] = s[:, :g] + corr
                    if q_lo + g < tq:
                        s_ref[:, q_lo + g:] = s[:, g:]
                staged.append((t, gi, q_lo, s_ref, (acc0_ref, acc1_ref)[c]))
        for t, gi, q_lo, s_ref, acc in staged:
            pv = _dot(vt_ref[gi], jnp.exp(s_ref[:, q_lo:]).astype(BF16))
            if init and t == 0:
                acc[...] = pv
            else:
                acc[:, q_lo:] += pv

    nsub = tq // g
    key = lax.broadcasted_iota(jnp.int32, (g, g), 0)
    qry = lax.broadcasted_iota(jnp.int32, (g, g), 1)
    ahead = key - qry
    visible = (key // CHUNK) <= (qry // CHUNK)
    corr = jnp.where(ahead <= 0, 0.0,
                     jnp.where(visible, (-2.0 * slope) * ahead.astype(F32), MASKED))
    for a in range(0, nsub, 2):
        key_tiles(((nsub * qi + a, a * g, corr), (nsub * qi + a + 1, (a + 1) * g, corr)), init=(a == 0))

    def body(jj, carry):
        key_tiles(((2 * jj, 0, None), (2 * jj + 1, 0, None)))
        return carry

    lax.fori_loop(0, qi * (nsub // 2), body, 0)

    a0 = acc0_ref[...]
    a1 = acc1_ref[...]
    ot = (a0[:DIFF_DV, :] / a0[DIFF_DV:DIFF_DV + 1, :]
          - lam * (a1[:DIFF_DV, :] / a1[DIFF_DV:DIFF_DV + 1, :]))
    ms = jnp.mean(ot * ot, axis=0, keepdims=True)
    yt = (ot * lax.rsqrt(ms + EPS) * subg_ref[...]) * out_scale
    o_ref[...] = yt.T.astype(BF16)


def _diff_attn(scal, qa, ka, vt, subg, *, batch, seq, tq):
    nq = seq // tq
    g = vt.shape[-1]
    assert tq % (2 * g) == 0
    return pl.pallas_call(
        _diff_attn_kernel,
        grid=(batch, DIFF_HEADS, nq),
        in_specs=[
            pl.BlockSpec(memory_space=pltpu.SMEM),
            pl.BlockSpec((tq, HEAD_W), lambda b, h, i: (b * nq + i, h)),
            pl.BlockSpec((seq, HEAD_W), lambda b, h, i: (b, h)),
            pl.BlockSpec((None, None, seq // g, V_ROWS, g), lambda b, h, i: (b, h, 0, 0, 0)),
            pl.BlockSpec((DIFF_DV, 1), lambda b, h, i: (0, 0)),
        ],
        out_specs=pl.BlockSpec((tq, DIFF_DV), lambda b, h, i: (b * nq + i, h)),
        out_shape=jax.ShapeDtypeStruct((batch * seq, DIFF_W), BF16),
        scratch_shapes=([pltpu.VMEM((V_ROWS, tq), F32)] * 2 + [pltpu.VMEM((g, tq), F32)] * 4),
        compiler_params=pltpu.CompilerParams(
            dimension_semantics=("arbitrary", "arbitrary", "arbitrary"),
            vmem_limit_bytes=VMEM_LIMIT_BYTES),
        name="diff_attn",
    )(scal, qa, ka, vt, subg)


def _gla_kernel(zg_ref, za_ref, aw_ref, ab_ref, ng_ref, lmat_ref, amask_ref, bd_ref,
                o_ref, st_ref):
    @pl.when(pl.program_id(0) == 0)
    def _():
        st_ref[...] = jnp.zeros_like(st_ref)

    nb, r = zg_ref.shape[0], zg_ref.shape[1]
    seqs = range(nb)
    lane = lax.broadcasted_iota(jnp.int32, (1, GLA_W), 1)
    rowid = lax.broadcasted_iota(jnp.int32, (r, 1), 0)
    bd = bd_ref[...]
    bd_f = bd.astype(F32)

    log_a = []
    for bb in seqs:
        pre = _dot(za_ref[bb], aw_ref[...]) + ab_ref[...]
        log_a.append((jnp.minimum(pre, 0.0) - jnp.log(1.0 + jnp.exp(-jnp.abs(pre)))) * (1.0 / GLA_TAU))
    q_in, k_in, k_end, dec = [], [], [], []
    for bb in seqs:
        hi = log_a[bb].astype(BF16)
        lo = (log_a[bb] - hi.astype(F32)).astype(BF16)
        cs = _dot(lmat_ref[...], jnp.concatenate([hi, lo], axis=1))
        b = cs[:r, :GLA_W] + cs[:r, GLA_W:]
        b_tot = cs[r:, :GLA_W] + cs[r:, GLA_W:]
        gq = zg_ref[bb, :, 0:GLA_W].astype(F32)
        gk = zg_ref[bb, :, GLA_W:2 * GLA_W].astype(F32)
        q_in.append(((gq * (GLA_DK ** -0.5)) * jnp.exp(b)).astype(BF16))
        k_in.append((gk * jnp.exp(-b)).astype(BF16))
        k_end.append(gk * jnp.exp(b_tot - b))
        dec.append(jnp.exp(b_tot))

    o = [jnp.zeros((r, GLA_W), F32) for _ in seqs]
    for hd in range(GLA_HEADS):
        in_head = (lane >= hd * GLA_DK) & (lane < (hd + 1) * GLA_DK)
        for bb in seqs:
            att = _nt_dot(jnp.where(in_head, q_in[bb], jnp.zeros_like(q_in[bb])), k_in[bb])
            att = jnp.where(amask_ref[...] > 0.0, att, 0.0).astype(BF16)
            o[bb] = jnp.where(in_head, _dot(att, zg_ref[bb, :, 2 * GLA_W:3 * GLA_W]), o[bb])

    kv_t = []
    for bb in seqs:
        gv_t = zg_ref[bb, :, 2 * GLA_W:3 * GLA_W].astype(F32).T.astype(BF16)
        per_chunk = []
        for c in range(r // CHUNK):
            in_chunk = (rowid >= c * CHUNK) & (rowid < (c + 1) * CHUNK)
            per_chunk.append(_dot(gv_t, jnp.where(in_chunk, k_end[bb], 0.0).astype(BF16)) * bd_f)
        kv_t.append(per_chunk)
    st = [st_ref[bb] for bb in seqs]
    inter = [[] for _ in seqs]
    for c in range(r // CHUNK):
        rows = slice(c * CHUNK, (c + 1) * CHUNK)
        for bb in seqs:
            inter[bb].append(_nt_dot(q_in[bb][rows, :], st[bb].astype(BF16)))
            st[bb] = st[bb] * dec[bb][c * CHUNK:c * CHUNK + 1, :] + kv_t[bb][c]
    for bb in seqs:
        st_ref[bb] = st[bb]

    for bb in seqs:
        ob = o[bb] + jnp.concatenate(inter[bb], axis=0)
        ssq = _dot((ob * ob).astype(BF16), bd)
        on = ob * lax.rsqrt(ssq * (1.0 / GLA_DV) + EPS) * ng_ref[...]
        gg = zg_ref[bb, :, 3 * GLA_W:4 * GLA_W].astype(F32)
        o_ref[bb] = (on * (gg * (1.0 / (1.0 + jnp.exp(-gg))))).astype(BF16)


def _gla(zg, za, aw, ab, ng, lmat, amask, bd, *, batch, seq, r):
    row = lambda t: (0, t, 0)
    fixed = lambda t: (0, 0)
    return pl.pallas_call(
        _gla_kernel,
        grid=(seq // r,),
        in_specs=[
            pl.BlockSpec((batch, r, 4 * GLA_W), row),
            pl.BlockSpec((batch, r, LANES), row),
            pl.BlockSpec((LANES, GLA_W), fixed),
            pl.BlockSpec((1, GLA_W), fixed),
            pl.BlockSpec((1, GLA_W), fixed),
            pl.BlockSpec((2 * r, r), fixed),
            pl.BlockSpec((r, r), fixed),
            pl.BlockSpec((GLA_W, GLA_W), fixed),
        ],
        out_specs=pl.BlockSpec((batch, r, GLA_W), row),
        out_shape=jax.ShapeDtypeStruct((batch, seq, GLA_W), BF16),
        scratch_shapes=[pltpu.VMEM((batch, GLA_W, GLA_W), F32)],
        compiler_params=pltpu.CompilerParams(
            dimension_semantics=("arbitrary",), vmem_limit_bytes=VMEM_LIMIT_BYTES),
        name="gla",
    )(zg, za, aw, ab, ng, lmat, amask, bd)


def _out_mlp_kernel(x_ref, yc_ref, yd_ref, yg_ref, wo_ref, g2_ref, w1_ref, w2_ref,
                    o_ref, h2_ref):
    f = pl.program_id(1)

    @pl.when(f == 0)
    def _():
        sub = x_ref.shape[0] // 4
        for p in range(4):
            rows = slice(p * sub, (p + 1) * sub)
            y = (_dot(yc_ref[rows, :], wo_ref[0:CONV_W, :])
                 + _dot(yd_ref[rows, :], wo_ref[CONV_W:CONV_W + DIFF_W, :])
                 + _dot(yg_ref[rows, :], wo_ref[CONV_W + DIFF_W:, :]))
            x1 = x_ref[rows, :] + y
            ms = jnp.mean(x1 * x1, axis=-1, keepdims=True)
            h2_ref[rows, :] = (x1 * lax.rsqrt(ms + EPS) * g2_ref[...]).astype(BF16)
            o_ref[rows, :] = x1

    a = jnp.maximum(_dot(h2_ref[...], w1_ref[...]), 0.0)
    o_ref[...] += _dot((a * a).astype(BF16), w2_ref[...])


def _out_mlp(x, yc, yd, yg, wo, g2, w1, w2, *, tm, tf, layer):
    t = x.shape[0]
    row = lambda i, f: (i, 0)
    fixed = lambda i, f: (0, 0)
    return pl.pallas_call(
        _out_mlp_kernel,
        grid=(t // tm, D_FF // tf),
        in_specs=[
            pl.BlockSpec((tm, D_MODEL), row),
            pl.BlockSpec((tm, CONV_W), row),
            pl.BlockSpec((tm, DIFF_W), row),
            pl.BlockSpec((tm, GLA_W), row),
            pl.BlockSpec((None, D_MODEL, D_MODEL), lambda i, f: (layer, 0, 0)),
            pl.BlockSpec((1, D_MODEL), fixed),
            pl.BlockSpec((None, D_MODEL, tf), lambda i, f: (layer, 0, f)),
            pl.BlockSpec((None, tf, D_MODEL), lambda i, f: (layer, f, 0)),
        ],
        out_specs=pl.BlockSpec((tm, D_MODEL), row),
        out_shape=jax.ShapeDtypeStruct((t, D_MODEL), F32),
        scratch_shapes=[pltpu.VMEM((tm, D_MODEL), BF16)],
        compiler_params=pltpu.CompilerParams(
            dimension_semantics=("arbitrary", "arbitrary"), vmem_limit_bytes=VMEM_LIMIT_BYTES),
        name="out_mlp",
    )(x, yc, yd, yg, wo, g2, w1, w2)


def _tile_sizes(batch, seq):
    tk = min(512, seq // 4)
    return dict(tk=tk, tm_in=2 * tk, tq=4 * tk, r_gla=min(256, seq), tm_mlp=min(1024, seq), tf=1024)


def _block_diag_ones(n, blk):
    idx = jnp.arange(n) // blk
    return idx[:, None] == idx[None, :]


def _position_columns(seq):
    width = DIFF_HEADS * AUG_W
    pos = lax.broadcasted_iota(jnp.int32, (seq, width), 0)
    lane = lax.broadcasted_iota(jnp.int32, (seq, width), 1)
    col = lane % AUG_W
    slope = jnp.asarray(np.repeat(
        np.asarray([2.0 ** (-8.0 * (hd + 1) / DIFF_HEADS) for hd in range(DIFF_HEADS)], np.float32),
        AUG_W))[None, :]
    hi = slope * (pos - pos % LANES).astype(F32)
    lo = slope * (pos % LANES).astype(F32)
    is_one_q = (col == 0) | (col == 3)
    is_one_k = (col == 1) | (col == 2) | (col == SHIFT_COL)
    augq = jnp.where(is_one_q, 1.0, jnp.where(col == 1, -hi, jnp.where(col == 2, -lo, 0.0)))
    augk = jnp.where(is_one_k, 1.0, jnp.where(col == 0, hi, jnp.where(col == 3, lo, 0.0)))
    return augq.astype(BF16), augk.astype(BF16)


def kernel(x, ln1_g, w_in, conv_w, q_norm_g, k_norm_g, diff_lambda, diff_subln_g, gla_alpha_w,
           gla_alpha_b, gla_norm_g, w_out, ln2_g, w_mlp1, w_mlp2):
    batch, seq, d_model = x.shape
    depth = w_in.shape[0]
    assert d_model == D_MODEL and seq % (2 * CHUNK) == 0
    ts = _tile_sizes(batch, seq)
    r = ts["r_gla"]

    bd_q = _block_diag_ones(DIFF_W, DIFF_DH).astype(BF16)
    bd_g = _block_diag_ones(GLA_W, GLA_DV).astype(BF16)
    same_chunk = _block_diag_ones(r, CHUNK)
    causal = jnp.arange(r)[:, None] >= jnp.arange(r)[None, :]
    lmat = jnp.concatenate([same_chunk & causal, same_chunk], axis=0).astype(BF16)
    amask = (same_chunk & causal).astype(F32)
    augq, augk = _position_columns(seq)

    w_all = w_in.astype(BF16)
    w_a = jnp.pad(w_in[:, :, D_MAIN:], ((0, 0), (0, 0), (0, LANES - GLA_RANK))).astype(BF16)
    a_w = jnp.pad(gla_alpha_w, ((0, 0), (0, LANES - GLA_RANK), (0, 0))).astype(BF16)
    conv_w8 = jnp.pad(conv_w, ((0, 0), (0, 8 - CONV_K), (0, 0)))
    w_o = w_out.astype(BF16)
    w_1 = w_mlp1.astype(BF16)
    w_2 = w_mlp2.astype(BF16)

    xf = x.reshape(batch * seq, D_MODEL)
    for l in range(depth):
        lam_init = 0.8 - 0.6 * math.exp(-0.3 * l)
        lp = diff_lambda[l].astype(F32)
        lam = jnp.exp(jnp.sum(lp[0] * lp[1])) - jnp.exp(jnp.sum(lp[2] * lp[3])) + lam_init
        slopes = [2.0 ** (-8.0 * (hd + 1) / DIFF_HEADS) for hd in range(DIFF_HEADS)]
        shift = (DIFF_DH ** 0.5) * jnp.max(jnp.abs(q_norm_g[l])) * jnp.max(jnp.abs(k_norm_g[l]))
        scal = jnp.stack([lam, jnp.asarray(1.0 - lam_init, F32), shift]
                         + [jnp.asarray(s, F32) for s in slopes]).astype(F32)
        qg = jnp.tile(q_norm_g[l] * (DIFF_DH ** -0.5), 2 * DIFF_HEADS)[None, :]
        kg = jnp.tile(k_norm_g[l], 2 * DIFF_HEADS)[None, :]

        yconv, qa, ka, vt, zg, za = _in_proj(
            xf, ln1_g[l][None, :], w_all, w_a[l], conv_w8[l], qg, kg, bd_q,
            augq, augk, seq=seq, tm=ts["tm_in"], g=ts["tk"], layer=l)
        ydiff = _diff_attn(scal, qa, ka, vt, diff_subln_g[l][:, None],
                           batch=batch, seq=seq, tq=ts["tq"])
        ygla = _gla(zg.reshape(batch, seq, 4 * GLA_W), za.reshape(batch, seq, LANES), a_w[l],
                    gla_alpha_b[l][None, :], jnp.tile(gla_norm_g[l], GLA_HEADS)[None, :],
                    lmat, amask, bd_g, batch=batch, seq=seq, r=r)
        xf = _out_mlp(xf, yconv, ydiff, ygla.reshape(batch * seq, GLA_W), w_o, ln2_g[l][None, :],
                      w_1, w_2, tm=ts["tm_mlp"], tf=ts["tf"], layer=l)
    return xf.reshape(batch, seq, D_MODEL)
```

```python
import functools
import math

import jax
import jax.numpy as jnp
import numpy as np
from jax import lax
from jax.experimental import pallas as pl
from jax.experimental.pallas import tpu as pltpu

F32 = jnp.float32
BF16 = jnp.bfloat16

D_MODEL = 1024
CHUNK = 64
EPS = 1e-6
CONV_W = 256
CONV_K = 3
DIFF_HEADS = 4
DIFF_DH = 64
DIFF_DV = 128
DIFF_W = DIFF_HEADS * DIFF_DV
GLA_HEADS = 4
GLA_DK = 64
GLA_DV = 64
GLA_W = GLA_HEADS * GLA_DV
GLA_RANK = 16
GLA_TAU = 16.0
D_FF = 4 * D_MODEL

_C_CONV = 0
_C_Q = 3 * CONV_W
_C_K = _C_Q + DIFF_HEADS * 2 * DIFF_DH
_C_V = _C_K + DIFF_HEADS * 2 * DIFF_DH
_C_G = _C_V + DIFF_W
_C_A = _C_G + 4 * GLA_W
D_MAIN = _C_A

LANES = 128
AUG_W = LANES
HEAD_W = 2 * DIFF_DH + AUG_W
SHIFT_COL = 4
V_ROWS = DIFF_DV + 16
MASKED = -1e30
VMEM_LIMIT_BYTES = 52 * 1024 * 1024


def _nt_dot(a, b):
    return lax.dot_general(a, b, (((1,), (1,)), ((), ())), preferred_element_type=F32)


def _dot(a, b):
    return jnp.dot(a, b, preferred_element_type=F32)


def _in_proj_kernel(x_ref, g1_ref, w_ref, wa_ref, convw_ref, qg_ref, kg_ref, bd_ref,
                    augq_ref, augk_ref,
                    yconv_ref, qa_ref, ka_ref, vt_ref, zg_ref, za_ref, carry_ref,
                    *, tiles_per_seq):
    i = pl.program_id(0)
    g = vt_ref.shape[-1]
    parts = [slice(p * g, (p + 1) * g) for p in range(x_ref.shape[0] // g)]

    @pl.when(i % tiles_per_seq == 0)
    def _():
        carry_ref[...] = jnp.zeros_like(carry_ref)

    prev = carry_ref[...]
    row = lax.broadcasted_iota(jnp.int32, (g, CONV_W), 0)
    cw = convw_ref[...]
    h = []
    for p, rows in enumerate(parts):
        x = x_ref[rows, :]
        ms = jnp.mean(x * x, axis=-1, keepdims=True)
        h.append((x * lax.rsqrt(ms + EPS) * g1_ref[...]).astype(BF16))
        zc = _dot(h[p], w_ref[:, _C_CONV:_C_Q])
        u, cb, cc = zc[:, :CONV_W], zc[:, CONV_W:2 * CONV_W], zc[:, 2 * CONV_W:]
        z = cc * u
        z1 = jnp.where(row == 0, prev[7:8, :], pltpu.roll(z, 1, 0))
        z2 = jnp.where(row == 0, prev[6:7, :],
                       jnp.where(row == 1, prev[7:8, :], pltpu.roll(z, 2, 0)))
        yconv_ref[rows, :] = (cb * (cw[0:1, :] * z2 + cw[1:2, :] * z1 + cw[2:3, :] * z)).astype(BF16)
        prev = z[g - 8:, :]
    carry_ref[...] = prev

    def seg_norm(zz, g_ref):
        ssq = _dot((zz * zz).astype(BF16), bd_ref[...])
        return zz * lax.rsqrt(ssq * (1.0 / DIFF_DH) + EPS) * g_ref[...]

    sub = lax.broadcasted_iota(jnp.int32, (V_ROWS - DIFF_DV, g), 0)
    ones_row = jnp.where(sub == 0, 1.0, 0.0).astype(BF16)
    for col0, g_ref, dst_ref, aug_ref in ((_C_Q, qg_ref, qa_ref, augq_ref), (_C_K, kg_ref, ka_ref, augk_ref)):
        for p, rows in enumerate(parts):
            zn = seg_norm(_dot(h[p], w_ref[:, col0:col0 + DIFF_W]), g_ref).astype(BF16)
            for hd in range(DIFF_HEADS):
                lo, mid, hi = hd * HEAD_W, hd * HEAD_W + 2 * DIFF_DH, (hd + 1) * HEAD_W
                dst_ref[rows, lo:mid] = zn[:, hd * 2 * DIFF_DH:(hd + 1) * 2 * DIFF_DH]
                dst_ref[rows, mid:hi] = aug_ref[rows, hd * AUG_W:(hd + 1) * AUG_W]
    for p, rows in enumerate(parts):
        zv = _dot(h[p], w_ref[:, _C_V:_C_G])
        for hd in range(DIFF_HEADS):
            vt_ref[hd, p, 0:DIFF_DV, :] = zv[:, hd * DIFF_DV:(hd + 1) * DIFF_DV].T.astype(BF16)
            vt_ref[hd, p, DIFF_DV:, :] = ones_row

    for p, rows in enumerate(parts):
        za_ref[rows, :] = _dot(h[p], wa_ref[...]).astype(BF16)
        zg_ref[rows, :] = _dot(h[p], w_ref[:, _C_G:_C_A]).astype(BF16)


def _in_proj(x, g1, w, wa, convw, qg, kg, bd, augq, augk, *, seq, tm, g, layer):
    t = x.shape[0]
    tps = seq // tm
    batch = t // seq
    row = lambda i: (i, 0)
    fixed = lambda i: (0, 0)
    pos = lambda i: (i % tps, 0)
    out_w = DIFF_HEADS * HEAD_W
    return pl.pallas_call(
        functools.partial(_in_proj_kernel, tiles_per_seq=tps),
        grid=(t // tm,),
        in_specs=[
            pl.BlockSpec((tm, D_MODEL), row),
            pl.BlockSpec((1, D_MODEL), fixed),
            pl.BlockSpec((None, D_MODEL, D_MAIN), lambda i: (layer, 0, 0)),
            pl.BlockSpec((D_MODEL, LANES), fixed),
            pl.BlockSpec((8, CONV_W), fixed),
            pl.BlockSpec((1, DIFF_W), fixed),
            pl.BlockSpec((1, DIFF_W), fixed),
            pl.BlockSpec((DIFF_W, DIFF_W), fixed),
            pl.BlockSpec((tm, DIFF_HEADS * AUG_W), pos),
            pl.BlockSpec((tm, DIFF_HEADS * AUG_W), pos),
        ],
        out_specs=[
            pl.BlockSpec((tm, CONV_W), row),
            pl.BlockSpec((tm, out_w), row),
            pl.BlockSpec((tm, out_w), row),
            pl.BlockSpec((None, DIFF_HEADS, tm // g, V_ROWS, g), lambda i: (i // tps, 0, i % tps, 0, 0)),
            pl.BlockSpec((tm, 4 * GLA_W), row),
            pl.BlockSpec((tm, LANES), row),
        ],
        out_shape=[
            jax.ShapeDtypeStruct((t, CONV_W), BF16),
            jax.ShapeDtypeStruct((t, out_w), BF16),
            jax.ShapeDtypeStruct((t, out_w), BF16),
            jax.ShapeDtypeStruct((batch, DIFF_HEADS, seq // g, V_ROWS, g), BF16),
            jax.ShapeDtypeStruct((t, 4 * GLA_W), BF16),
            jax.ShapeDtypeStruct((t, LANES), BF16),
        ],
        scratch_shapes=[pltpu.VMEM((8, CONV_W), F32)],
        compiler_params=pltpu.CompilerParams(
            dimension_semantics=("arbitrary",), vmem_limit_bytes=VMEM_LIMIT_BYTES),
        name="in_proj",
    )(x, g1, w, wa, convw, qg, kg, bd, augq, augk)


def _diff_attn_kernel(scal_ref, q_ref, k_ref, vt_ref, subg_ref, o_ref, acc0_ref, acc1_ref, *s_refs):
    hd = pl.program_id(1)
    qi = pl.program_id(2)
    tq = q_ref.shape[0]
    g = vt_ref.shape[-1]
    lam = scal_ref[0]
    out_scale = scal_ref[1]
    shift = scal_ref[2]
    slope = scal_ref[3 + hd]

    qt = q_ref[...].astype(F32).T
    feat = lax.broadcasted_iota(jnp.int32, qt.shape, 0)
    qt = jnp.where(feat == 2 * DIFF_DH + SHIFT_COL, -shift, qt)
    qt0 = jnp.where((feat >= DIFF_DH) & (feat < 2 * DIFF_DH), 0.0, qt).astype(BF16)
    qt1 = jnp.where(feat < DIFF_DH, 0.0, qt).astype(BF16)

    def key_tiles(jobs, init=False):
        staged = []
        for t, (gi, q_lo, corr) in enumerate(jobs):
            kt = k_ref[pl.ds(pl.multiple_of(gi * g, g), g), :]
            for c, qtc in enumerate((qt0, qt1)):
                s_ref = s_refs[2 * t + c]
                s = _dot(kt, qtc[:, q_lo:])
                if corr is None:
                    s_ref[:, q_lo:] = s
                else:
                    s_ref[:, q_lo:q_lo + g] = s[:, :g] + corr
                    if q_lo + g < tq:
                        s_ref[:, q_lo + g:] = s[:, g:]
                staged.append((t, gi, q_lo, s_ref, (acc0_ref, acc1_ref)[c]))
        for t, gi, q_lo, s_ref, acc in staged:
            pv = _dot(vt_ref[gi], jnp.exp(s_ref[:, q_lo:]).astype(BF16))
            if init and t == 0:
                acc[...] = pv
            else:
                acc[:, q_lo:] += pv

    nsub = tq // g
    key = lax.broadcasted_iota(jnp.int32, (g, g), 0)
    qry = lax.broadcasted_iota(jnp.int32, (g, g), 1)
    ahead = key - qry
    visible = (key // CHUNK) <= (qry // CHUNK)
    corr = jnp.where(ahead <= 0, 0.0,
                     jnp.where(visible, (-2.0 * slope) * ahead.astype(F32), MASKED))
    for a in range(0, nsub, 2):
        key_tiles(((nsub * qi + a, a * g, corr), (nsub * qi + a + 1, (a + 1) * g, corr)), init=(a == 0))

    def body(jj, carry):
        kt = k_ref[pl.ds(pl.multiple_of(2 * jj * g, 2 * g), 2 * g), :]
        for c, qtc in enumerate((qt0, qt1)):
            s = _dot(kt, qtc)
            s_refs[c][...] = s[:g, :]
            s_refs[2 + c][...] = s[g:, :]
        for c, acc in enumerate((acc0_ref, acc1_ref)):
            acc[...] += (_dot(vt_ref[2 * jj], jnp.exp(s_refs[c][...]).astype(BF16))
                         + _dot(vt_ref[2 * jj + 1], jnp.exp(s_refs[2 + c][...]).astype(BF16)))
        return carry

    lax.fori_loop(0, qi * (nsub // 2), body, 0)

    a0 = acc0_ref[...]
    a1 = acc1_ref[...]
    ot = (a0[:DIFF_DV, :] / a0[DIFF_DV:DIFF_DV + 1, :]
          - lam * (a1[:DIFF_DV, :] / a1[DIFF_DV:DIFF_DV + 1, :]))
    ms = jnp.mean(ot * ot, axis=0, keepdims=True)
    yt = (ot * lax.rsqrt(ms + EPS) * subg_ref[...]) * out_scale
    o_ref[...] = yt.T.astype(BF16)


def _diff_attn(scal, qa, ka, vt, subg, *, batch, seq, tq):
    nq = seq // tq
    g = vt.shape[-1]
    assert tq % (2 * g) == 0
    return pl.pallas_call(
        _diff_attn_kernel,
        grid=(batch, DIFF_HEADS, nq),
        in_specs=[
            pl.BlockSpec(memory_space=pltpu.SMEM),
            pl.BlockSpec((tq, HEAD_W), lambda b, h, i: (b * nq + i, h)),
            pl.BlockSpec((seq, HEAD_W), lambda b, h, i: (b, h)),
            pl.BlockSpec((None, None, seq // g, V_ROWS, g), lambda b, h, i: (b, h, 0, 0, 0)),
            pl.BlockSpec((DIFF_DV, 1), lambda b, h, i: (0, 0)),
        ],
        out_specs=pl.BlockSpec((tq, DIFF_DV), lambda b, h, i: (b * nq + i, h)),
        out_shape=jax.ShapeDtypeStruct((batch * seq, DIFF_W), BF16),
        scratch_shapes=([pltpu.VMEM((V_ROWS, tq), F32)] * 2 + [pltpu.VMEM((g, tq), F32)] * 4),
        compiler_params=pltpu.CompilerParams(
            dimension_semantics=("arbitrary", "arbitrary", "arbitrary"),
            vmem_limit_bytes=VMEM_LIMIT_BYTES),
        name="diff_attn",
    )(scal, qa, ka, vt, subg)


def _gla_kernel(zg_ref, za_ref, aw_ref, ab_ref, ng_ref, lmat_ref, amask_ref, bd_ref,
                o_ref, st_ref):
    @pl.when(pl.program_id(0) == 0)
    def _():
        st_ref[...] = jnp.zeros_like(st_ref)

    nb, r = zg_ref.shape[0], zg_ref.shape[1]
    seqs = range(nb)
    lane = lax.broadcasted_iota(jnp.int32, (1, GLA_W), 1)
    rowid = lax.broadcasted_iota(jnp.int32, (r, 1), 0)
    bd = bd_ref[...]
    bd_f = bd.astype(F32)

    log_a = []
    for bb in seqs:
        pre = _dot(za_ref[bb], aw_ref[...]) + ab_ref[...]
        log_a.append((jnp.minimum(pre, 0.0) - jnp.log(1.0 + jnp.exp(-jnp.abs(pre)))) * (1.0 / GLA_TAU))
    q_in, k_in, k_end, dec = [], [], [], []
    for bb in seqs:
        hi = log_a[bb].astype(BF16)
        lo = (log_a[bb] - hi.astype(F32)).astype(BF16)
        cs = _dot(lmat_ref[...], jnp.concatenate([hi, lo], axis=1))
        b = cs[:, :GLA_W] + cs[:, GLA_W:]
        b_last = [b[(c + 1) * CHUNK - 1:(c + 1) * CHUNK, :] for c in range(r // CHUNK)]
        b_tot = jnp.concatenate([jnp.broadcast_to(bl, (CHUNK, GLA_W)) for bl in b_last], axis=0)
        gq = zg_ref[bb, :, 0:GLA_W].astype(F32)
        gk = zg_ref[bb, :, GLA_W:2 * GLA_W].astype(F32)
        q_in.append(((gq * (GLA_DK ** -0.5)) * jnp.exp(b)).astype(BF16))
        k_in.append((gk * jnp.exp(-b)).astype(BF16))
        k_end.append(gk * jnp.exp(b_tot - b))
        dec.append([jnp.exp(bl) for bl in b_last])

    o = [jnp.zeros((r, GLA_W), F32) for _ in seqs]
    for hd in range(GLA_HEADS):
        in_head = (lane >= hd * GLA_DK) & (lane < (hd + 1) * GLA_DK)
        for bb in seqs:
            att = _nt_dot(jnp.where(in_head, q_in[bb], jnp.zeros_like(q_in[bb])), k_in[bb])
            att = jnp.where(amask_ref[...] > 0.0, att, 0.0).astype(BF16)
            o[bb] = jnp.where(in_head, _dot(att, zg_ref[bb, :, 2 * GLA_W:3 * GLA_W]), o[bb])

    kv_t = []
    for bb in seqs:
        gv_t = zg_ref[bb, :, 2 * GLA_W:3 * GLA_W].astype(F32).T.astype(BF16)
        per_chunk = []
        for c in range(r // CHUNK):
            in_chunk = (rowid >= c * CHUNK) & (rowid < (c + 1) * CHUNK)
            per_chunk.append(_dot(gv_t, jnp.where(in_chunk, k_end[bb], 0.0).astype(BF16)) * bd_f)
        kv_t.append(per_chunk)
    st = [st_ref[bb] for bb in seqs]
    inter = [[] for _ in seqs]
    for c in range(r // CHUNK):
        rows = slice(c * CHUNK, (c + 1) * CHUNK)
        for bb in seqs:
            inter[bb].append(_nt_dot(q_in[bb][rows, :], st[bb].astype(BF16)))
            st[bb] = st[bb] * dec[bb][c] + kv_t[bb][c]
    for bb in seqs:
        st_ref[bb] = st[bb]

    for bb in seqs:
        ob = o[bb] + jnp.concatenate(inter[bb], axis=0)
        ssq = _dot((ob * ob).astype(BF16), bd)
        on = ob * lax.rsqrt(ssq * (1.0 / GLA_DV) + EPS) * ng_ref[...]
        gg = zg_ref[bb, :, 3 * GLA_W:4 * GLA_W].astype(F32)
        o_ref[bb] = (on * (gg * (1.0 / (1.0 + jnp.exp(-gg))))).astype(BF16)


def _gla(zg, za, aw, ab, ng, lmat, amask, bd, *, batch, seq, r):
    row = lambda t: (0, t, 0)
    fixed = lambda t: (0, 0)
    return pl.pallas_call(
        _gla_kernel,
        grid=(seq // r,),
        in_specs=[
            pl.BlockSpec((batch, r, 4 * GLA_W), row),
            pl.BlockSpec((batch, r, LANES), row),
            pl.BlockSpec((LANES, GLA_W), fixed),
            pl.BlockSpec((1, GLA_W), fixed),
            pl.BlockSpec((1, GLA_W), fixed),
            pl.BlockSpec((r, r), fixed),
            pl.BlockSpec((r, r), fixed),
            pl.BlockSpec((GLA_W, GLA_W), fixed),
        ],
        out_specs=pl.BlockSpec((batch, r, GLA_W), row),
        out_shape=jax.ShapeDtypeStruct((batch, seq, GLA_W), BF16),
        scratch_shapes=[pltpu.VMEM((batch, GLA_W, GLA_W), F32)],
        compiler_params=pltpu.CompilerParams(
            dimension_semantics=("arbitrary",), vmem_limit_bytes=VMEM_LIMIT_BYTES),
        name="gla",
    )(zg, za, aw, ab, ng, lmat, amask, bd)


def _out_mlp_kernel(x_ref, yc_ref, yd_ref, yg_ref, wo_ref, g2_ref, w1_ref, w2_ref,
                    o_ref, h2_ref):
    f = pl.program_id(1)

    @pl.when(f == 0)
    def _():
        sub = x_ref.shape[0] // 4
        for p in range(4):
            rows = slice(p * sub, (p + 1) * sub)
            y = (_dot(yc_ref[rows, :], wo_ref[0:CONV_W, :])
                 + _dot(yd_ref[rows, :], wo_ref[CONV_W:CONV_W + DIFF_W, :])
                 + _dot(yg_ref[rows, :], wo_ref[CONV_W + DIFF_W:, :]))
            x1 = x_ref[rows, :] + y
            ms = jnp.mean(x1 * x1, axis=-1, keepdims=True)
            h2_ref[rows, :] = (x1 * lax.rsqrt(ms + EPS) * g2_ref[...]).astype(BF16)
            o_ref[rows, :] = x1

    a = jnp.maximum(_dot(h2_ref[...], w1_ref[...]), 0.0)
    o_ref[...] += _dot((a * a).astype(BF16), w2_ref[...])


def _out_mlp(x, yc, yd, yg, wo, g2, w1, w2, *, tm, tf, layer):
    t = x.shape[0]
    row = lambda i, f: (i, 0)
    fixed = lambda i, f: (0, 0)
    return pl.pallas_call(
        _out_mlp_kernel,
        grid=(t // tm, D_FF // tf),
        in_specs=[
            pl.BlockSpec((tm, D_MODEL), row),
            pl.BlockSpec((tm, CONV_W), row),
            pl.BlockSpec((tm, DIFF_W), row),
            pl.BlockSpec((tm, GLA_W), row),
            pl.BlockSpec((None, D_MODEL, D_MODEL), lambda i, f: (layer, 0, 0)),
            pl.BlockSpec((1, D_MODEL), fixed),
            pl.BlockSpec((None, D_MODEL, tf), lambda i, f: (layer, 0, f)),
            pl.BlockSpec((None, tf, D_MODEL), lambda i, f: (layer, f, 0)),
        ],
        out_specs=pl.BlockSpec((tm, D_MODEL), row),
        out_shape=jax.ShapeDtypeStruct((t, D_MODEL), F32),
        scratch_shapes=[pltpu.VMEM((tm, D_MODEL), BF16)],
        compiler_params=pltpu.CompilerParams(
            dimension_semantics=("arbitrary", "arbitrary"), vmem_limit_bytes=VMEM_LIMIT_BYTES),
        name="out_mlp",
    )(x, yc, yd, yg, wo, g2, w1, w2)


def _tile_sizes(batch, seq):
    tk = min(512, seq // 4)
    return dict(tk=tk, tm_in=2 * tk, tq=4 * tk, r_gla=min(256, seq), tm_mlp=min(1024, seq), tf=1024)


def _block_diag_ones(n, blk):
    idx = jnp.arange(n) // blk
    return idx[:, None] == idx[None, :]


def _position_columns(seq):
    width = DIFF_HEADS * AUG_W
    pos = lax.broadcasted_iota(jnp.int32, (seq, width), 0)
    lane = lax.broadcasted_iota(jnp.int32, (seq, width), 1)
    col = lane % AUG_W
    slope = jnp.asarray(np.repeat(
        np.asarray([2.0 ** (-8.0 * (hd + 1) / DIFF_HEADS) for hd in range(DIFF_HEADS)], np.float32),
        AUG_W))[None, :]
    hi = slope * (pos - pos % LANES).astype(F32)
    lo = slope * (pos % LANES).astype(F32)
    is_one_q = (col == 0) | (col == 3)
    is_one_k = (col == 1) | (col == 2) | (col == SHIFT_COL)
    augq = jnp.where(is_one_q, 1.0, jnp.where(col == 1, -hi, jnp.where(col == 2, -lo, 0.0)))
    augk = jnp.where(is_one_k, 1.0, jnp.where(col == 0, hi, jnp.where(col == 3, lo, 0.0)))
    return augq.astype(BF16), augk.astype(BF16)


def kernel(x, ln1_g, w_in, conv_w, q_norm_g, k_norm_g, diff_lambda, diff_subln_g, gla_alpha_w,
           gla_alpha_b, gla_norm_g, w_out, ln2_g, w_mlp1, w_mlp2):
    batch, seq, d_model = x.shape
    depth = w_in.shape[0]
    assert d_model == D_MODEL and seq % (2 * CHUNK) == 0
    ts = _tile_sizes(batch, seq)
    r = ts["r_gla"]

    bd_q = _block_diag_ones(DIFF_W, DIFF_DH).astype(BF16)
    bd_g = _block_diag_ones(GLA_W, GLA_DV).astype(BF16)
    same_chunk = _block_diag_ones(r, CHUNK)
    causal = jnp.arange(r)[:, None] >= jnp.arange(r)[None, :]
    lmat = (same_chunk & causal).astype(BF16)
    amask = (same_chunk & causal).astype(F32)
    augq, augk = _position_columns(seq)

    w_all = w_in.astype(BF16)
    w_a = jnp.pad(w_in[:, :, D_MAIN:], ((0, 0), (0, 0), (0, LANES - GLA_RANK))).astype(BF16)
    a_w = jnp.pad(gla_alpha_w, ((0, 0), (0, LANES - GLA_RANK), (0, 0))).astype(BF16)
    conv_w8 = jnp.pad(conv_w, ((0, 0), (0, 8 - CONV_K), (0, 0)))
    w_o = w_out.astype(BF16)
    w_1 = w_mlp1.astype(BF16)
    w_2 = w_mlp2.astype(BF16)

    xf = x.reshape(batch * seq, D_MODEL)
    for l in range(depth):
        lam_init = 0.8 - 0.6 * math.exp(-0.3 * l)
        lp = diff_lambda[l].astype(F32)
        lam = jnp.exp(jnp.sum(lp[0] * lp[1])) - jnp.exp(jnp.sum(lp[2] * lp[3])) + lam_init
        slopes = [2.0 ** (-8.0 * (hd + 1) / DIFF_HEADS) for hd in range(DIFF_HEADS)]
        shift = (DIFF_DH ** 0.5) * jnp.max(jnp.abs(q_norm_g[l])) * jnp.max(jnp.abs(k_norm_g[l]))
        scal = jnp.stack([lam, jnp.asarray(1.0 - lam_init, F32), shift]
                         + [jnp.asarray(s, F32) for s in slopes]).astype(F32)
        qg = jnp.tile(q_norm_g[l] * (DIFF_DH ** -0.5), 2 * DIFF_HEADS)[None, :]
        kg = jnp.tile(k_norm_g[l], 2 * DIFF_HEADS)[None, :]

        yconv, qa, ka, vt, zg, za = _in_proj(
            xf, ln1_g[l][None, :], w_all, w_a[l], conv_w8[l], qg, kg, bd_q,
            augq, augk, seq=seq, tm=ts["tm_in"], g=ts["tk"], layer=l)
        ydiff = _diff_attn(scal, qa, ka, vt, diff_subln_g[l][:, None],
                           batch=batch, seq=seq, tq=ts["tq"])
        ygla = _gla(zg.reshape(batch, seq, 4 * GLA_W), za.reshape(batch, seq, LANES), a_w[l],
                    gla_alpha_b[l][None, :], jnp.tile(gla_norm_g[l], GLA_HEADS)[None, :],
                    lmat, amask, bd_g, batch=batch, seq=seq, r=r)
        xf = _out_mlp(xf, yconv, ydiff, ygla.reshape(batch * seq, GLA_W), w_o, ln2_g[l][None, :],
                      w_1, w_2, tm=ts["tm_mlp"], tf=ts["tf"], layer=l)
    return xf.reshape(batch, seq, D_MODEL)
```

```python
import functools
import math

import jax
import jax.numpy as jnp
import numpy as np
from jax import lax
from jax.experimental import pallas as pl
from jax.experimental.pallas import tpu as pltpu

F32 = jnp.float32
BF16 = jnp.bfloat16

D_MODEL = 1024
CHUNK = 64
EPS = 1e-6
CONV_W = 256
CONV_K = 3
DIFF_HEADS = 4
DIFF_DH = 64
DIFF_DV = 128
DIFF_W = DIFF_HEADS * DIFF_DV
GLA_HEADS = 4
GLA_DK = 64
GLA_DV = 64
GLA_W = GLA_HEADS * GLA_DV
GLA_RANK = 16
GLA_TAU = 16.0
D_FF = 4 * D_MODEL

_C_CONV = 0
_C_Q = 3 * CONV_W
_C_K = _C_Q + DIFF_HEADS * 2 * DIFF_DH
_C_V = _C_K + DIFF_HEADS * 2 * DIFF_DH
_C_G = _C_V + DIFF_W
_C_A = _C_G + 4 * GLA_W
D_MAIN = _C_A

LANES = 128
AUG_W = LANES
HEAD_W = 2 * DIFF_DH + AUG_W
LOG2E = math.log2(math.e)
MAX_EXP2_ARG = 100.0
V_ROWS = DIFF_DV + 16
MASKED = -1e30
VMEM_LIMIT_BYTES = 52 * 1024 * 1024


def _nt_dot(a, b):
    return lax.dot_general(a, b, (((1,), (1,)), ((), ())), preferred_element_type=F32)


def _dot(a, b):
    return jnp.dot(a, b, preferred_element_type=F32)


def _in_proj_kernel(x_ref, g1_ref, w_ref, wa_ref, convw_ref, qg_ref, kg_ref, bd_ref,
                    augq_ref, augk_ref,
                    yconv_ref, qa_ref, ka_ref, vt_ref, zg_ref, za_ref, carry_ref,
                    *, tiles_per_seq):
    i = pl.program_id(0)
    g = vt_ref.shape[-1]
    parts = [slice(p * g, (p + 1) * g) for p in range(x_ref.shape[0] // g)]

    @pl.when(i % tiles_per_seq == 0)
    def _():
        carry_ref[...] = jnp.zeros_like(carry_ref)

    prev = carry_ref[...]
    row = lax.broadcasted_iota(jnp.int32, (g, CONV_W), 0)
    cw = convw_ref[...]
    h = []
    for p, rows in enumerate(parts):
        x = x_ref[rows, :]
        ms = jnp.mean(x * x, axis=-1, keepdims=True)
        h.append((x * lax.rsqrt(ms + EPS) * g1_ref[...]).astype(BF16))
        zc = _dot(h[p], w_ref[:, _C_CONV:_C_Q])
        u, cb, cc = zc[:, :CONV_W], zc[:, CONV_W:2 * CONV_W], zc[:, 2 * CONV_W:]
        z = cc * u
        z1 = jnp.where(row == 0, prev[7:8, :], pltpu.roll(z, 1, 0))
        z2 = jnp.where(row == 0, prev[6:7, :],
                       jnp.where(row == 1, prev[7:8, :], pltpu.roll(z, 2, 0)))
        yconv_ref[rows, :] = (cb * (cw[0:1, :] * z2 + cw[1:2, :] * z1 + cw[2:3, :] * z)).astype(BF16)
        prev = z[g - 8:, :]
    carry_ref[...] = prev

    def seg_norm(zz, g_ref):
        ssq = _dot((zz * zz).astype(BF16), bd_ref[...])
        return zz * lax.rsqrt(ssq * (1.0 / DIFF_DH) + EPS) * g_ref[...]

    sub = lax.broadcasted_iota(jnp.int32, (V_ROWS - DIFF_DV, g), 0)
    ones_row = jnp.where(sub == 0, 1.0, 0.0).astype(BF16)
    for col0, g_ref, dst_ref, aug_ref in ((_C_Q, qg_ref, qa_ref, augq_ref), (_C_K, kg_ref, ka_ref, augk_ref)):
        for p, rows in enumerate(parts):
            zn = seg_norm(_dot(h[p], w_ref[:, col0:col0 + DIFF_W]), g_ref).astype(BF16)
            for hd in range(DIFF_HEADS):
                lo, mid, hi = hd * HEAD_W, hd * HEAD_W + 2 * DIFF_DH, (hd + 1) * HEAD_W
                dst_ref[rows, lo:mid] = zn[:, hd * 2 * DIFF_DH:(hd + 1) * 2 * DIFF_DH]
                dst_ref[rows, mid:hi] = aug_ref[rows, hd * AUG_W:(hd + 1) * AUG_W]
    for p, rows in enumerate(parts):
        zv = _dot(h[p], w_ref[:, _C_V:_C_G])
        for hd in range(DIFF_HEADS):
            vt_ref[hd, p, 0:DIFF_DV, :] = zv[:, hd * DIFF_DV:(hd + 1) * DIFF_DV].T.astype(BF16)
            vt_ref[hd, p, DIFF_DV:, :] = ones_row

    for p, rows in enumerate(parts):
        za_ref[rows, :] = _dot(h[p], wa_ref[...]).astype(BF16)
        zg_ref[rows, :] = _dot(h[p], w_ref[:, _C_G:_C_A]).astype(BF16)


def _in_proj(x, g1, w, wa, convw, qg, kg, bd, augq, augk, *, seq, tm, g, layer):
    t = x.shape[0]
    tps = seq // tm
    batch = t // seq
    row = lambda i: (i, 0)
    fixed = lambda i: (0, 0)
    pos = lambda i: (i % tps, 0)
    out_w = DIFF_HEADS * HEAD_W
    return pl.pallas_call(
        functools.partial(_in_proj_kernel, tiles_per_seq=tps),
        grid=(t // tm,),
        in_specs=[
            pl.BlockSpec((tm, D_MODEL), row),
            pl.BlockSpec((1, D_MODEL), fixed),
            pl.BlockSpec((None, D_MODEL, D_MAIN), lambda i: (layer, 0, 0)),
            pl.BlockSpec((D_MODEL, LANES), fixed),
            pl.BlockSpec((8, CONV_W), fixed),
            pl.BlockSpec((1, DIFF_W), fixed),
            pl.BlockSpec((1, DIFF_W), fixed),
            pl.BlockSpec((DIFF_W, DIFF_W), fixed),
            pl.BlockSpec((tm, DIFF_HEADS * AUG_W), pos),
            pl.BlockSpec((tm, DIFF_HEADS * AUG_W), pos),
        ],
        out_specs=[
            pl.BlockSpec((tm, CONV_W), row),
            pl.BlockSpec((tm, out_w), row),
            pl.BlockSpec((tm, out_w), row),
            pl.BlockSpec((None, DIFF_HEADS, tm // g, V_ROWS, g), lambda i: (i // tps, 0, i % tps, 0, 0)),
            pl.BlockSpec((tm, 4 * GLA_W), row),
            pl.BlockSpec((tm, LANES), row),
        ],
        out_shape=[
            jax.ShapeDtypeStruct((t, CONV_W), BF16),
            jax.ShapeDtypeStruct((t, out_w), BF16),
            jax.ShapeDtypeStruct((t, out_w), BF16),
            jax.ShapeDtypeStruct((batch, DIFF_HEADS, seq // g, V_ROWS, g), BF16),
            jax.ShapeDtypeStruct((t, 4 * GLA_W), BF16),
            jax.ShapeDtypeStruct((t, LANES), BF16),
        ],
        scratch_shapes=[pltpu.VMEM((8, CONV_W), F32)],
        compiler_params=pltpu.CompilerParams(
            dimension_semantics=("arbitrary",), vmem_limit_bytes=VMEM_LIMIT_BYTES),
        name="in_proj",
    )(x, g1, w, wa, convw, qg, kg, bd, augq, augk)


def _diff_attn_kernel(scal_ref, q_ref, k_ref, vt_ref, subg_ref, o_ref,
                      acc0_ref, acc1_ref, m0_ref, m1_ref, *s_refs):
    hd = pl.program_id(1)
    qi = pl.program_id(2)
    tq = q_ref.shape[0]
    g = vt_ref.shape[-1]
    lam = scal_ref[0]
    out_scale = scal_ref[1]
    slope2 = scal_ref[2 + hd]

    qt = q_ref[...].astype(F32).T
    feat = lax.broadcasted_iota(jnp.int32, qt.shape, 0)
    qt = jnp.where(feat < 2 * DIFF_DH, qt * LOG2E, qt)
    qt0 = jnp.where((feat >= DIFF_DH) & (feat < 2 * DIFF_DH), 0.0, qt).astype(BF16)
    qt1 = jnp.where(feat < DIFF_DH, 0.0, qt).astype(BF16)
    accs = (acc0_ref, acc1_ref)
    ms = (m0_ref, m1_ref)

    def key_tiles(jobs, init=False):
        m_cur = [None if init else m_ref[...] for m_ref in ms]
        staged = []
        for t, (gi, q_lo, corr) in enumerate(jobs):
            kt = k_ref[pl.ds(pl.multiple_of(gi * g, g), g), :]
            for c, qtc in enumerate((qt0, qt1)):
                s_ref = s_refs[2 * t + c]
                s = _dot(kt, qtc[:, q_lo:])
                s_d = s[:, :g] + corr
                s_ref[:, q_lo:q_lo + g] = s_d
                m_tile = jnp.max(s_d, axis=0, keepdims=True)
                if q_lo + g < tq:
                    s_ref[:, q_lo + g:] = s[:, g:]
                    m_tile = jnp.concatenate([m_tile, jnp.max(s[:, g:], axis=0, keepdims=True)], axis=1)
                if m_cur[c] is None:
                    m_new, alpha = m_tile, None
                else:
                    m_new = jnp.maximum(m_cur[c][:, q_lo:], m_tile)
                    alpha = jnp.exp2(m_cur[c][:, q_lo:] - m_new)
                    if q_lo:
                        m_new_full = jnp.concatenate([m_cur[c][:, :q_lo], m_new], axis=1)
                    else:
                        m_new_full = m_new
                staged.append((gi, q_lo, s_ref, accs[c], m_new, alpha))
                m_cur[c] = m_new if alpha is None else m_new_full
        for gi, q_lo, s_ref, acc, m_new, alpha in staged:
            pv = _dot(vt_ref[gi], jnp.exp2(s_ref[:, q_lo:] - m_new).astype(BF16))
            if alpha is None:
                acc[...] = pv
            else:
                acc[:, q_lo:] = acc[:, q_lo:] * alpha + pv
        for c, m_ref in enumerate(ms):
            m_ref[...] = m_cur[c]

    nsub = tq // g
    key = lax.broadcasted_iota(jnp.int32, (g, g), 0)
    qry = lax.broadcasted_iota(jnp.int32, (g, g), 1)
    ahead = key - qry
    visible = (key // CHUNK) <= (qry // CHUNK)
    corr = jnp.where(ahead <= 0, 0.0,
                     jnp.where(visible, (-2.0 * slope2) * ahead.astype(F32), MASKED))
    for a in range(0, nsub, 2):
        key_tiles(((nsub * qi + a, a * g, corr), (nsub * qi + a + 1, (a + 1) * g, corr)), init=(a == 0))

    def make_body(track_max):
        def body(jj, carry):
            kt = k_ref[pl.ds(pl.multiple_of(2 * jj * g, 2 * g), 2 * g), :]
            state = []
            for c, qtc in enumerate((qt0, qt1)):
                s = _dot(kt, qtc)
                s_refs[c][...] = s[:g, :]
                s_refs[2 + c][...] = s[g:, :]
                m_old = ms[c][...]
                if track_max:
                    m_new = jnp.maximum(m_old, jnp.max(s, axis=0, keepdims=True))
                    state.append((m_new, jnp.exp2(m_old - m_new)))
                else:
                    state.append((m_old, None))
            for c, acc in enumerate(accs):
                m_new, alpha = state[c]
                pv = (_dot(vt_ref[2 * jj], jnp.exp2(s_refs[c][...] - m_new).astype(BF16))
                      + _dot(vt_ref[2 * jj + 1], jnp.exp2(s_refs[2 + c][...] - m_new).astype(BF16)))
                if track_max:
                    acc[...] = acc[...] * alpha + pv
                    ms[c][...] = m_new
                else:
                    acc[...] += pv
            return carry
        return body

    bound_is_safe = scal_ref[2 + DIFF_HEADS] > 0.5

    @pl.when(bound_is_safe)
    def _():
        lax.fori_loop(0, qi * (nsub // 2), make_body(False), 0)

    @pl.when(jnp.logical_not(bound_is_safe))
    def _():
        lax.fori_loop(0, qi * (nsub // 2), make_body(True), 0)

    a0 = acc0_ref[...]
    a1 = acc1_ref[...]
    ot = (a0[:DIFF_DV, :] / a0[DIFF_DV:DIFF_DV + 1, :]
          - lam * (a1[:DIFF_DV, :] / a1[DIFF_DV:DIFF_DV + 1, :]))
    ms = jnp.mean(ot * ot, axis=0, keepdims=True)
    yt = (ot * lax.rsqrt(ms + EPS) * subg_ref[...]) * out_scale
    o_ref[...] = yt.T.astype(BF16)


def _diff_attn(scal, qa, ka, vt, subg, *, batch, seq, tq):
    nq = seq // tq
    g = vt.shape[-1]
    assert tq % (2 * g) == 0
    return pl.pallas_call(
        _diff_attn_kernel,
        grid=(batch, DIFF_HEADS, nq),
        in_specs=[
            pl.BlockSpec(memory_space=pltpu.SMEM),
            pl.BlockSpec((tq, HEAD_W), lambda b, h, i: (b * nq + i, h)),
            pl.BlockSpec((seq, HEAD_W), lambda b, h, i: (b, h)),
            pl.BlockSpec((None, None, seq // g, V_ROWS, g), lambda b, h, i: (b, h, 0, 0, 0)),
            pl.BlockSpec((DIFF_DV, 1), lambda b, h, i: (0, 0)),
        ],
        out_specs=pl.BlockSpec((tq, DIFF_DV), lambda b, h, i: (b * nq + i, h)),
        out_shape=jax.ShapeDtypeStruct((batch * seq, DIFF_W), BF16),
        scratch_shapes=([pltpu.VMEM((V_ROWS, tq), F32)] * 2 + [pltpu.VMEM((1, tq), F32)] * 2
                        + [pltpu.VMEM((g, tq), F32)] * 4),
        compiler_params=pltpu.CompilerParams(
            dimension_semantics=("arbitrary", "arbitrary", "arbitrary"),
            vmem_limit_bytes=VMEM_LIMIT_BYTES),
        name="diff_attn",
    )(scal, qa, ka, vt, subg)


def _gla_kernel(zg_ref, za_ref, aw_ref, ab_ref, ng_ref, lmat_ref, amask_ref, bd_ref,
                o_ref, st_ref):
    @pl.when(pl.program_id(0) == 0)
    def _():
        st_ref[...] = jnp.zeros_like(st_ref)

    nb, r = zg_ref.shape[0], zg_ref.shape[1]
    seqs = range(nb)
    lane = lax.broadcasted_iota(jnp.int32, (1, GLA_W), 1)
    rowid = lax.broadcasted_iota(jnp.int32, (r, 1), 0)
    bd = bd_ref[...]
    bd_f = bd.astype(F32)

    log_a = []
    for bb in seqs:
        pre = _dot(za_ref[bb], aw_ref[...]) + ab_ref[...]
        log_a.append((jnp.minimum(pre, 0.0) - jnp.log(1.0 + jnp.exp(-jnp.abs(pre)))) * (1.0 / GLA_TAU))
    q_in, k_in, k_end, dec = [], [], [], []
    for bb in seqs:
        hi = log_a[bb].astype(BF16)
        lo = (log_a[bb] - hi.astype(F32)).astype(BF16)
        cs = _dot(lmat_ref[...], jnp.concatenate([hi, lo], axis=1))
        b = cs[:, :GLA_W] + cs[:, GLA_W:]
        b_last = [b[(c + 1) * CHUNK - 1:(c + 1) * CHUNK, :] for c in range(r // CHUNK)]
        b_tot = jnp.concatenate([jnp.broadcast_to(bl, (CHUNK, GLA_W)) for bl in b_last], axis=0)
        gq = zg_ref[bb, :, 0:GLA_W].astype(F32)
        gk = zg_ref[bb, :, GLA_W:2 * GLA_W].astype(F32)
        q_in.append(((gq * (GLA_DK ** -0.5)) * jnp.exp(b)).astype(BF16))
        k_in.append((gk * jnp.exp(-b)).astype(BF16))
        k_end.append(gk * jnp.exp(b_tot - b))
        dec.append([jnp.exp(bl) for bl in b_last])

    o = [jnp.zeros((r, GLA_W), F32) for _ in seqs]
    for hd in range(GLA_HEADS):
        in_head = (lane >= hd * GLA_DK) & (lane < (hd + 1) * GLA_DK)
        for bb in seqs:
            att = _nt_dot(jnp.where(in_head, q_in[bb], jnp.zeros_like(q_in[bb])), k_in[bb])
            att = jnp.where(amask_ref[...] > 0.0, att, 0.0).astype(BF16)
            o[bb] = jnp.where(in_head, _dot(att, zg_ref[bb, :, 2 * GLA_W:3 * GLA_W]), o[bb])

    kv_t = []
    for bb in seqs:
        gv_t = zg_ref[bb, :, 2 * GLA_W:3 * GLA_W].astype(F32).T.astype(BF16)
        per_chunk = []
        for c in range(r // CHUNK):
            in_chunk = (rowid >= c * CHUNK) & (rowid < (c + 1) * CHUNK)
            per_chunk.append(_dot(gv_t, jnp.where(in_chunk, k_end[bb], 0.0).astype(BF16)) * bd_f)
        kv_t.append(per_chunk)
    st = [st_ref[bb] for bb in seqs]
    inter = [[] for _ in seqs]
    for c in range(r // CHUNK):
        rows = slice(c * CHUNK, (c + 1) * CHUNK)
        for bb in seqs:
            inter[bb].append(_nt_dot(q_in[bb][rows, :], st[bb].astype(BF16)))
            st[bb] = st[bb] * dec[bb][c] + kv_t[bb][c]
    for bb in seqs:
        st_ref[bb] = st[bb]

    for bb in seqs:
        ob = o[bb] + jnp.concatenate(inter[bb], axis=0)
        ssq = _dot((ob * ob).astype(BF16), bd)
        on = ob * lax.rsqrt(ssq * (1.0 / GLA_DV) + EPS) * ng_ref[...]
        gg = zg_ref[bb, :, 3 * GLA_W:4 * GLA_W].astype(F32)
        o_ref[bb] = (on * (gg * (1.0 / (1.0 + jnp.exp(-gg))))).astype(BF16)


def _gla(zg, za, aw, ab, ng, lmat, amask, bd, *, batch, seq, r):
    row = lambda t: (0, t, 0)
    fixed = lambda t: (0, 0)
    return pl.pallas_call(
        _gla_kernel,
        grid=(seq // r,),
        in_specs=[
            pl.BlockSpec((batch, r, 4 * GLA_W), row),
            pl.BlockSpec((batch, r, LANES), row),
            pl.BlockSpec((LANES, GLA_W), fixed),
            pl.BlockSpec((1, GLA_W), fixed),
            pl.BlockSpec((1, GLA_W), fixed),
            pl.BlockSpec((r, r), fixed),
            pl.BlockSpec((r, r), fixed),
            pl.BlockSpec((GLA_W, GLA_W), fixed),
        ],
        out_specs=pl.BlockSpec((batch, r, GLA_W), row),
        out_shape=jax.ShapeDtypeStruct((batch, seq, GLA_W), BF16),
        scratch_shapes=[pltpu.VMEM((batch, GLA_W, GLA_W), F32)],
        compiler_params=pltpu.CompilerParams(
            dimension_semantics=("arbitrary",), vmem_limit_bytes=VMEM_LIMIT_BYTES),
        name="gla",
    )(zg, za, aw, ab, ng, lmat, amask, bd)


def _out_mlp_kernel(x_ref, yc_ref, yd_ref, yg_ref, wo_ref, g2_ref, w1_ref, w2_ref,
                    o_ref, h2_ref):
    f = pl.program_id(1)

    @pl.when(f == 0)
    def _():
        sub = x_ref.shape[0] // 4
        for p in range(4):
            rows = slice(p * sub, (p + 1) * sub)
            y = (_dot(yc_ref[rows, :], wo_ref[0:CONV_W, :])
                 + _dot(yd_ref[rows, :], wo_ref[CONV_W:CONV_W + DIFF_W, :])
                 + _dot(yg_ref[rows, :], wo_ref[CONV_W + DIFF_W:, :]))
            x1 = x_ref[rows, :] + y
            ms = jnp.mean(x1 * x1, axis=-1, keepdims=True)
            h2_ref[rows, :] = (x1 * lax.rsqrt(ms + EPS) * g2_ref[...]).astype(BF16)
            o_ref[rows, :] = x1

    a = jnp.maximum(_dot(h2_ref[...], w1_ref[...]), 0.0)
    o_ref[...] += _dot((a * a).astype(BF16), w2_ref[...])


def _out_mlp(x, yc, yd, yg, wo, g2, w1, w2, *, tm, tf, layer):
    t = x.shape[0]
    row = lambda i, f: (i, 0)
    fixed = lambda i, f: (0, 0)
    return pl.pallas_call(
        _out_mlp_kernel,
        grid=(t // tm, D_FF // tf),
        in_specs=[
            pl.BlockSpec((tm, D_MODEL), row),
            pl.BlockSpec((tm, CONV_W), row),
            pl.BlockSpec((tm, DIFF_W), row),
            pl.BlockSpec((tm, GLA_W), row),
            pl.BlockSpec((None, D_MODEL, D_MODEL), lambda i, f: (layer, 0, 0)),
            pl.BlockSpec((1, D_MODEL), fixed),
            pl.BlockSpec((None, D_MODEL, tf), lambda i, f: (layer, 0, f)),
            pl.BlockSpec((None, tf, D_MODEL), lambda i, f: (layer, f, 0)),
        ],
        out_specs=pl.BlockSpec((tm, D_MODEL), row),
        out_shape=jax.ShapeDtypeStruct((t, D_MODEL), F32),
        scratch_shapes=[pltpu.VMEM((tm, D_MODEL), BF16)],
        compiler_params=pltpu.CompilerParams(
            dimension_semantics=("arbitrary", "arbitrary"), vmem_limit_bytes=VMEM_LIMIT_BYTES),
        name="out_mlp",
    )(x, yc, yd, yg, wo, g2, w1, w2)


def _tile_sizes(batch, seq):
    tk = min(512, seq // 4)
    return dict(tk=tk, tm_in=2 * tk, tq=4 * tk, r_gla=min(256, seq), tm_mlp=min(1024, seq), tf=1024)


def _block_diag_ones(n, blk):
    idx = jnp.arange(n) // blk
    return idx[:, None] == idx[None, :]


def _alibi_slopes2():
    return np.asarray([2.0 ** (-8.0 * (hd + 1) / DIFF_HEADS) * LOG2E for hd in range(DIFF_HEADS)],
                      np.float32)


def _position_columns(seq):
    width = DIFF_HEADS * AUG_W
    pos = lax.broadcasted_iota(jnp.int32, (seq, width), 0)
    col = lax.broadcasted_iota(jnp.int32, (seq, width), 1) % AUG_W
    c = jnp.asarray(np.repeat(_alibi_slopes2(), AUG_W))[None, :]
    c_hi = c.astype(BF16).astype(F32)
    c_lo = c - c_hi
    block = (pos - pos % LANES).astype(F32)
    rem = (pos % LANES).astype(F32)
    augq = jnp.where((col == 0) | (col == 2), c_hi,
                     jnp.where((col == 1) | (col == 3), c_lo,
                               jnp.where(col == 4, -c * pos.astype(F32), 0.0)))
    augk = jnp.where(col < 2, block, jnp.where(col < 4, rem, jnp.where(col == 4, 1.0, 0.0)))
    return augq.astype(BF16), augk.astype(BF16)


def kernel(x, ln1_g, w_in, conv_w, q_norm_g, k_norm_g, diff_lambda, diff_subln_g, gla_alpha_w,
           gla_alpha_b, gla_norm_g, w_out, ln2_g, w_mlp1, w_mlp2):
    batch, seq, d_model = x.shape
    depth = w_in.shape[0]
    assert d_model == D_MODEL and seq % (2 * CHUNK) == 0
    ts = _tile_sizes(batch, seq)
    r = ts["r_gla"]

    bd_q = _block_diag_ones(DIFF_W, DIFF_DH).astype(BF16)
    bd_g = _block_diag_ones(GLA_W, GLA_DV).astype(BF16)
    same_chunk = _block_diag_ones(r, CHUNK)
    causal = jnp.arange(r)[:, None] >= jnp.arange(r)[None, :]
    lmat = (same_chunk & causal).astype(BF16)
    amask = (same_chunk & causal).astype(F32)
    augq, augk = _position_columns(seq)

    w_all = w_in.astype(BF16)
    w_a = jnp.pad(w_in[:, :, D_MAIN:], ((0, 0), (0, 0), (0, LANES - GLA_RANK))).astype(BF16)
    a_w = jnp.pad(gla_alpha_w, ((0, 0), (0, LANES - GLA_RANK), (0, 0))).astype(BF16)
    conv_w8 = jnp.pad(conv_w, ((0, 0), (0, 8 - CONV_K), (0, 0)))
    w_o = w_out.astype(BF16)
    w_1 = w_mlp1.astype(BF16)
    w_2 = w_mlp2.astype(BF16)

    xf = x.reshape(batch * seq, D_MODEL)
    for l in range(depth):
        lam_init = 0.8 - 0.6 * math.exp(-0.3 * l)
        lp = diff_lambda[l].astype(F32)
        lam = jnp.exp(jnp.sum(lp[0] * lp[1])) - jnp.exp(jnp.sum(lp[2] * lp[3])) + lam_init
        bound2 = 1.02 * LOG2E * (DIFF_DH ** 0.5) * jnp.max(jnp.abs(q_norm_g[l])) * jnp.max(jnp.abs(k_norm_g[l]))
        bound_is_safe = (2.0 * bound2 <= MAX_EXP2_ARG).astype(F32)
        scal = jnp.concatenate([jnp.stack([lam, jnp.asarray(1.0 - lam_init, F32)]),
                                jnp.asarray(_alibi_slopes2()), bound_is_safe[None]]).astype(F32)
        qg = jnp.tile(q_norm_g[l] * (DIFF_DH ** -0.5), 2 * DIFF_HEADS)[None, :]
        kg = jnp.tile(k_norm_g[l], 2 * DIFF_HEADS)[None, :]

        yconv, qa, ka, vt, zg, za = _in_proj(
            xf, ln1_g[l][None, :], w_all, w_a[l], conv_w8[l], qg, kg, bd_q,
            augq, augk, seq=seq, tm=ts["tm_in"], g=ts["tk"], layer=l)
        ydiff = _diff_attn(scal, qa, ka, vt, diff_subln_g[l][:, None],
                           batch=batch, seq=seq, tq=ts["tq"])
        ygla = _gla(zg.reshape(batch, seq, 4 * GLA_W), za.reshape(batch, seq, LANES), a_w[l],
                    gla_alpha_b[l][None, :], jnp.tile(gla_norm_g[l], GLA_HEADS)[None, :],
                    lmat, amask, bd_g, batch=batch, seq=seq, r=r)
        xf = _out_mlp(xf, yconv, ydiff, ygla.reshape(batch * seq, GLA_W), w_o, ln2_g[l][None, :],
                      w_1, w_2, tm=ts["tm_mlp"], tf=ts["tf"], layer=l)
    return xf.reshape(batch, seq, D_MODEL)
```

```python
import functools
import math

import jax
import jax.numpy as jnp
import numpy as np
from jax import lax
from jax.experimental import pallas as pl
from jax.experimental.pallas import tpu as pltpu

F32 = jnp.float32
BF16 = jnp.bfloat16

D_MODEL = 1024
CHUNK = 64
EPS = 1e-6
CONV_W = 256
CONV_K = 3
DIFF_HEADS = 4
DIFF_DH = 64
DIFF_DV = 128
DIFF_W = DIFF_HEADS * DIFF_DV
GLA_HEADS = 4
GLA_DK = 64
GLA_DV = 64
GLA_W = GLA_HEADS * GLA_DV
GLA_RANK = 16
GLA_TAU = 16.0
D_FF = 4 * D_MODEL

_C_CONV = 0
_C_Q = 3 * CONV_W
_C_K = _C_Q + DIFF_HEADS * 2 * DIFF_DH
_C_V = _C_K + DIFF_HEADS * 2 * DIFF_DH
_C_G = _C_V + DIFF_W
_C_A = _C_G + 4 * GLA_W
D_MAIN = _C_A

LANES = 128
AUG_W = LANES
HEAD_W = 2 * DIFF_DH + AUG_W
LOG2E = math.log2(math.e)
MAX_EXP2_ARG = 100.0
GLA_SEQS_PER_STEP = 4
V_ROWS = DIFF_DV + 16
MASKED = -1e30
VMEM_LIMIT_BYTES = 52 * 1024 * 1024

SUBLANES = 8
ATTN_KEY_TILE = 512
GLA_ROWS = 256
MLP_ROWS = 1024
MLP_HIDDEN_TILE = 1024
MLP_SUBTILES = 4


def _nt_dot(a, b):
    return lax.dot_general(a, b, (((1,), (1,)), ((), ())), preferred_element_type=F32)


def _dot(a, b):
    return jnp.dot(a, b, preferred_element_type=F32)


def _in_proj_kernel(x_ref, g1_ref, w_ref, wa_ref, convw_ref, qg_ref, kg_ref, bd_ref,
                    augq_ref, augk_ref,
                    yconv_ref, qa_ref, ka_ref, vt_ref, zg_ref, za_ref, carry_ref,
                    *, tiles_per_seq):
    i = pl.program_id(0)
    g = vt_ref.shape[-1]
    parts = [slice(p * g, (p + 1) * g) for p in range(x_ref.shape[0] // g)]

    @pl.when(i % tiles_per_seq == 0)
    def _():
        carry_ref[...] = jnp.zeros_like(carry_ref)

    prev = carry_ref[...]
    row = lax.broadcasted_iota(jnp.int32, (g, CONV_W), 0)
    cw = convw_ref[...]
    h = []
    for p, rows in enumerate(parts):
        x = x_ref[rows, :]
        ms = jnp.mean(x * x, axis=-1, keepdims=True)
        h.append((x * lax.rsqrt(ms + EPS) * g1_ref[...]).astype(BF16))
        zc = _dot(h[p], w_ref[:, _C_CONV:_C_Q])
        u, cb, cc = zc[:, :CONV_W], zc[:, CONV_W:2 * CONV_W], zc[:, 2 * CONV_W:]
        z = cc * u
        last, before_last = prev[SUBLANES - 1:SUBLANES, :], prev[SUBLANES - 2:SUBLANES - 1, :]
        z1 = jnp.where(row == 0, last, pltpu.roll(z, 1, 0))
        z2 = jnp.where(row == 0, before_last, jnp.where(row == 1, last, pltpu.roll(z, 2, 0)))
        yconv_ref[rows, :] = (cb * (cw[0:1, :] * z2 + cw[1:2, :] * z1 + cw[2:3, :] * z)).astype(BF16)
        prev = z[g - SUBLANES:, :]
    carry_ref[...] = prev

    def seg_norm(zz, g_ref):
        ssq = _dot((zz * zz).astype(BF16), bd_ref[...])
        return zz * lax.rsqrt(ssq * (1.0 / DIFF_DH) + EPS) * g_ref[...]

    sub = lax.broadcasted_iota(jnp.int32, (V_ROWS - DIFF_DV, g), 0)
    ones_row = jnp.where(sub == 0, 1.0, 0.0).astype(BF16)
    for col0, g_ref, dst_ref, aug_ref in ((_C_Q, qg_ref, qa_ref, augq_ref), (_C_K, kg_ref, ka_ref, augk_ref)):
        for p, rows in enumerate(parts):
            zn = seg_norm(_dot(h[p], w_ref[:, col0:col0 + DIFF_W]), g_ref).astype(BF16)
            for hd in range(DIFF_HEADS):
                lo, mid, hi = hd * HEAD_W, hd * HEAD_W + 2 * DIFF_DH, (hd + 1) * HEAD_W
                dst_ref[rows, lo:mid] = zn[:, hd * 2 * DIFF_DH:(hd + 1) * 2 * DIFF_DH]
                dst_ref[rows, mid:hi] = aug_ref[rows, hd * AUG_W:(hd + 1) * AUG_W]
    for p, rows in enumerate(parts):
        zv = _dot(h[p], w_ref[:, _C_V:_C_G])
        for hd in range(DIFF_HEADS):
            vt_ref[hd, p, 0:DIFF_DV, :] = zv[:, hd * DIFF_DV:(hd + 1) * DIFF_DV].T.astype(BF16)
            vt_ref[hd, p, DIFF_DV:, :] = ones_row

    for p, rows in enumerate(parts):
        za_ref[rows, :] = _dot(h[p], wa_ref[...]).astype(BF16)
        zg_ref[rows, :] = _dot(h[p], w_ref[:, _C_G:_C_A]).astype(BF16)


def _in_proj(x, g1, w, wa, convw, qg, kg, bd, augq, augk, *, seq, tm, g, layer):
    t = x.shape[0]
    tps = seq // tm
    batch = t // seq
    row = lambda i: (i, 0)
    fixed = lambda i: (0, 0)
    pos = lambda i: (i % tps, 0)
    out_w = DIFF_HEADS * HEAD_W
    return pl.pallas_call(
        functools.partial(_in_proj_kernel, tiles_per_seq=tps),
        grid=(t // tm,),
        in_specs=[
            pl.BlockSpec((tm, D_MODEL), row),
            pl.BlockSpec((1, D_MODEL), fixed),
            pl.BlockSpec((None, D_MODEL, D_MAIN), lambda i: (layer, 0, 0)),
            pl.BlockSpec((D_MODEL, LANES), fixed),
            pl.BlockSpec((SUBLANES, CONV_W), fixed),
            pl.BlockSpec((1, DIFF_W), fixed),
            pl.BlockSpec((1, DIFF_W), fixed),
            pl.BlockSpec((DIFF_W, DIFF_W), fixed),
            pl.BlockSpec((tm, DIFF_HEADS * AUG_W), pos),
            pl.BlockSpec((tm, DIFF_HEADS * AUG_W), pos),
        ],
        out_specs=[
            pl.BlockSpec((tm, CONV_W), row),
            pl.BlockSpec((tm, out_w), row),
            pl.BlockSpec((tm, out_w), row),
            pl.BlockSpec((None, DIFF_HEADS, tm // g, V_ROWS, g), lambda i: (i // tps, 0, i % tps, 0, 0)),
            pl.BlockSpec((tm, 4 * GLA_W), row),
            pl.BlockSpec((tm, LANES), row),
        ],
        out_shape=[
            jax.ShapeDtypeStruct((t, CONV_W), BF16),
            jax.ShapeDtypeStruct((t, out_w), BF16),
            jax.ShapeDtypeStruct((t, out_w), BF16),
            jax.ShapeDtypeStruct((batch, DIFF_HEADS, seq // g, V_ROWS, g), BF16),
            jax.ShapeDtypeStruct((t, 4 * GLA_W), BF16),
            jax.ShapeDtypeStruct((t, LANES), BF16),
        ],
        scratch_shapes=[pltpu.VMEM((SUBLANES, CONV_W), F32)],
        compiler_params=pltpu.CompilerParams(
            dimension_semantics=("arbitrary",), vmem_limit_bytes=VMEM_LIMIT_BYTES),
        name="in_proj",
    )(x, g1, w, wa, convw, qg, kg, bd, augq, augk)


def _diff_attn_kernel(scal_ref, q_ref, k_ref, vt_ref, subg_ref, o_ref,
                      acc0_ref, acc1_ref, m0_ref, m1_ref, *s_refs):
    hd = pl.program_id(1)
    qi = pl.program_id(2)
    tq = q_ref.shape[0]
    g = vt_ref.shape[-1]
    lam = scal_ref[0]
    out_scale = scal_ref[1]
    slope2 = scal_ref[2 + hd]

    qt = q_ref[...].astype(F32).T
    feat = lax.broadcasted_iota(jnp.int32, qt.shape, 0)
    qt = jnp.where(feat < 2 * DIFF_DH, qt * LOG2E, qt)
    qt0 = jnp.where((feat >= DIFF_DH) & (feat < 2 * DIFF_DH), 0.0, qt).astype(BF16)
    qt1 = jnp.where(feat < DIFF_DH, 0.0, qt).astype(BF16)
    accs = (acc0_ref, acc1_ref)
    ms = (m0_ref, m1_ref)

    def key_tiles(jobs, init=False):
        m_cur = [None if init else m_ref[...] for m_ref in ms]
        staged = []
        for t, (gi, q_lo, corr) in enumerate(jobs):
            kt = k_ref[pl.ds(pl.multiple_of(gi * g, g), g), :]
            for c, qtc in enumerate((qt0, qt1)):
                s_ref = s_refs[2 * t + c]
                s = _dot(kt, qtc[:, q_lo:])
                s_d = s[:, :g] + corr
                s_ref[:, q_lo:q_lo + g] = s_d
                m_tile = jnp.max(s_d, axis=0, keepdims=True)
                if q_lo + g < tq:
                    s_ref[:, q_lo + g:] = s[:, g:]
                    m_tile = jnp.concatenate([m_tile, jnp.max(s[:, g:], axis=0, keepdims=True)], axis=1)
                if m_cur[c] is None:
                    m_new, alpha = m_tile, None
                else:
                    m_new = jnp.maximum(m_cur[c][:, q_lo:], m_tile)
                    alpha = jnp.exp2(m_cur[c][:, q_lo:] - m_new)
                    if q_lo:
                        m_new_full = jnp.concatenate([m_cur[c][:, :q_lo], m_new], axis=1)
                    else:
                        m_new_full = m_new
                staged.append((gi, q_lo, s_ref, accs[c], m_new, alpha))
                m_cur[c] = m_new if alpha is None else m_new_full
        for gi, q_lo, s_ref, acc, m_new, alpha in staged:
            pv = _dot(vt_ref[gi], jnp.exp2(s_ref[:, q_lo:] - m_new).astype(BF16))
            if alpha is None:
                acc[...] = pv
            else:
                acc[:, q_lo:] = acc[:, q_lo:] * alpha + pv
        for c, m_ref in enumerate(ms):
            m_ref[...] = m_cur[c]

    nsub = tq // g
    key = lax.broadcasted_iota(jnp.int32, (g, g), 0)
    qry = lax.broadcasted_iota(jnp.int32, (g, g), 1)
    ahead = key - qry
    visible = (key // CHUNK) <= (qry // CHUNK)
    corr = jnp.where(ahead <= 0, 0.0,
                     jnp.where(visible, (-2.0 * slope2) * ahead.astype(F32), MASKED))
    for a in range(0, nsub, 2):
        key_tiles(((nsub * qi + a, a * g, corr), (nsub * qi + a + 1, (a + 1) * g, corr)), init=(a == 0))

    def make_body(track_max):
        def body(jj, carry):
            kt = k_ref[pl.ds(pl.multiple_of(2 * jj * g, 2 * g), 2 * g), :]
            state = []
            for c, qtc in enumerate((qt0, qt1)):
                s = _dot(kt, qtc)
                s_refs[c][...] = s[:g, :]
                s_refs[2 + c][...] = s[g:, :]
                m_old = ms[c][...]
                if track_max:
                    m_new = jnp.maximum(m_old, jnp.max(s, axis=0, keepdims=True))
                    state.append((m_new, jnp.exp2(m_old - m_new)))
                else:
                    state.append((m_old, None))
            for c, acc in enumerate(accs):
                m_new, alpha = state[c]
                pv = (_dot(vt_ref[2 * jj], jnp.exp2(s_refs[c][...] - m_new).astype(BF16))
                      + _dot(vt_ref[2 * jj + 1], jnp.exp2(s_refs[2 + c][...] - m_new).astype(BF16)))
                if track_max:
                    acc[...] = acc[...] * alpha + pv
                    ms[c][...] = m_new
                else:
                    acc[...] += pv
            return carry
        return body

    bound_is_safe = scal_ref[2 + DIFF_HEADS] > 0.5

    @pl.when(bound_is_safe)
    def _():
        lax.fori_loop(0, qi * (nsub // 2), make_body(False), 0)

    @pl.when(jnp.logical_not(bound_is_safe))
    def _():
        lax.fori_loop(0, qi * (nsub // 2), make_body(True), 0)

    a0 = acc0_ref[...]
    a1 = acc1_ref[...]
    ot = (a0[:DIFF_DV, :] / a0[DIFF_DV:DIFF_DV + 1, :]
          - lam * (a1[:DIFF_DV, :] / a1[DIFF_DV:DIFF_DV + 1, :]))
    mean_sq = jnp.mean(ot * ot, axis=0, keepdims=True)
    yt = (ot * lax.rsqrt(mean_sq + EPS) * subg_ref[...]) * out_scale
    o_ref[...] = yt.T.astype(BF16)


def _diff_attn(scal, qa, ka, vt, subg, *, batch, seq, tq):
    nq = seq // tq
    g = vt.shape[-1]
    assert tq % (2 * g) == 0
    return pl.pallas_call(
        _diff_attn_kernel,
        grid=(batch, DIFF_HEADS, nq),
        in_specs=[
            pl.BlockSpec(memory_space=pltpu.SMEM),
            pl.BlockSpec((tq, HEAD_W), lambda b, h, i: (b * nq + i, h)),
            pl.BlockSpec((seq, HEAD_W), lambda b, h, i: (b, h)),
            pl.BlockSpec((None, None, seq // g, V_ROWS, g), lambda b, h, i: (b, h, 0, 0, 0)),
            pl.BlockSpec((DIFF_DV, 1), lambda b, h, i: (0, 0)),
        ],
        out_specs=pl.BlockSpec((tq, DIFF_DV), lambda b, h, i: (b * nq + i, h)),
        out_shape=jax.ShapeDtypeStruct((batch * seq, DIFF_W), BF16),
        scratch_shapes=([pltpu.VMEM((V_ROWS, tq), F32)] * 2 + [pltpu.VMEM((1, tq), F32)] * 2
                        + [pltpu.VMEM((g, tq), F32)] * 4),
        compiler_params=pltpu.CompilerParams(
            dimension_semantics=("arbitrary", "arbitrary", "arbitrary"),
            vmem_limit_bytes=VMEM_LIMIT_BYTES),
        name="diff_attn",
    )(scal, qa, ka, vt, subg)


def _gla_kernel(zg_ref, za_ref, aw_ref, ab_ref, ng_ref, lmat_ref, amask_ref, bd_ref,
                o_ref, st_ref):
    @pl.when(pl.program_id(1) == 0)
    def _():
        st_ref[...] = jnp.zeros_like(st_ref)

    nb, r = zg_ref.shape[0], zg_ref.shape[1]
    seqs = range(nb)
    lane = lax.broadcasted_iota(jnp.int32, (1, GLA_W), 1)
    rowid = lax.broadcasted_iota(jnp.int32, (r, 1), 0)
    bd = bd_ref[...]
    bd_f = bd.astype(F32)

    log_a = []
    for bb in seqs:
        pre = _dot(za_ref[bb], aw_ref[...]) + ab_ref[...]
        log_a.append((jnp.minimum(pre, 0.0) - jnp.log(1.0 + jnp.exp(-jnp.abs(pre)))) * (1.0 / GLA_TAU))
    q_in, k_in, k_end, dec = [], [], [], []
    for bb in seqs:
        hi = log_a[bb].astype(BF16)
        lo = (log_a[bb] - hi.astype(F32)).astype(BF16)
        cs = _dot(lmat_ref[...], jnp.concatenate([hi, lo], axis=1))
        b = cs[:, :GLA_W] + cs[:, GLA_W:]
        b_last = [b[(c + 1) * CHUNK - 1:(c + 1) * CHUNK, :] for c in range(r // CHUNK)]
        b_tot = jnp.concatenate([jnp.broadcast_to(bl, (CHUNK, GLA_W)) for bl in b_last], axis=0)
        gq = zg_ref[bb, :, 0:GLA_W].astype(F32)
        gk = zg_ref[bb, :, GLA_W:2 * GLA_W].astype(F32)
        q_in.append(((gq * (GLA_DK ** -0.5)) * jnp.exp(b)).astype(BF16))
        k_in.append((gk * jnp.exp(-b)).astype(BF16))
        k_end.append(gk * jnp.exp(b_tot - b))
        dec.append([jnp.exp(bl) for bl in b_last])

    o = [jnp.zeros((r, GLA_W), F32) for _ in seqs]
    for hd in range(GLA_HEADS):
        in_head = (lane >= hd * GLA_DK) & (lane < (hd + 1) * GLA_DK)
        for bb in seqs:
            att = _nt_dot(jnp.where(in_head, q_in[bb], jnp.zeros_like(q_in[bb])), k_in[bb])
            att = jnp.where(amask_ref[...] > 0.0, att, 0.0).astype(BF16)
            o[bb] = jnp.where(in_head, _dot(att, zg_ref[bb, :, 2 * GLA_W:3 * GLA_W]), o[bb])

    kv_t = []
    for bb in seqs:
        gv_t = zg_ref[bb, :, 2 * GLA_W:3 * GLA_W].astype(F32).T.astype(BF16)
        per_chunk = []
        for c in range(r // CHUNK):
            in_chunk = (rowid >= c * CHUNK) & (rowid < (c + 1) * CHUNK)
            per_chunk.append(_dot(gv_t, jnp.where(in_chunk, k_end[bb], 0.0).astype(BF16)) * bd_f)
        kv_t.append(per_chunk)
    st = [st_ref[bb] for bb in seqs]
    inter = [[] for _ in seqs]
    for c in range(r // CHUNK):
        rows = slice(c * CHUNK, (c + 1) * CHUNK)
        for bb in seqs:
            inter[bb].append(_nt_dot(q_in[bb][rows, :], st[bb].astype(BF16)))
            st[bb] = st[bb] * dec[bb][c] + kv_t[bb][c]
    for bb in seqs:
        st_ref[bb] = st[bb]

    for bb in seqs:
        ob = o[bb] + jnp.concatenate(inter[bb], axis=0)
        ssq = _dot((ob * ob).astype(BF16), bd)
        on = ob * lax.rsqrt(ssq * (1.0 / GLA_DV) + EPS) * ng_ref[...]
        gg = zg_ref[bb, :, 3 * GLA_W:4 * GLA_W].astype(F32)
        o_ref[bb] = (on * (gg * (1.0 / (1.0 + jnp.exp(-gg))))).astype(BF16)


def _gla(zg, za, aw, ab, ng, lmat, amask, bd, *, batch, seq, r):
    nb = min(GLA_SEQS_PER_STEP, batch)
    row = lambda bg, t: (bg, t, 0)
    fixed = lambda bg, t: (0, 0)
    return pl.pallas_call(
        _gla_kernel,
        grid=(batch // nb, seq // r),
        in_specs=[
            pl.BlockSpec((nb, r, 4 * GLA_W), row),
            pl.BlockSpec((nb, r, LANES), row),
            pl.BlockSpec((LANES, GLA_W), fixed),
            pl.BlockSpec((1, GLA_W), fixed),
            pl.BlockSpec((1, GLA_W), fixed),
            pl.BlockSpec((r, r), fixed),
            pl.BlockSpec((r, r), fixed),
            pl.BlockSpec((GLA_W, GLA_W), fixed),
        ],
        out_specs=pl.BlockSpec((nb, r, GLA_W), row),
        out_shape=jax.ShapeDtypeStruct((batch, seq, GLA_W), BF16),
        scratch_shapes=[pltpu.VMEM((nb, GLA_W, GLA_W), F32)],
        compiler_params=pltpu.CompilerParams(
            dimension_semantics=("arbitrary", "arbitrary"), vmem_limit_bytes=VMEM_LIMIT_BYTES),
        name="gla",
    )(zg, za, aw, ab, ng, lmat, amask, bd)


def _out_mlp_kernel(x_ref, yc_ref, yd_ref, yg_ref, wo_ref, g2_ref, w1_ref, w2_ref,
                    o_ref, h2_ref):
    f = pl.program_id(1)

    @pl.when(f == 0)
    def _():
        sub = x_ref.shape[0] // MLP_SUBTILES
        for p in range(MLP_SUBTILES):
            rows = slice(p * sub, (p + 1) * sub)
            y = (_dot(yc_ref[rows, :], wo_ref[0:CONV_W, :])
                 + _dot(yd_ref[rows, :], wo_ref[CONV_W:CONV_W + DIFF_W, :])
                 + _dot(yg_ref[rows, :], wo_ref[CONV_W + DIFF_W:, :]))
            x1 = x_ref[rows, :] + y
            ms = jnp.mean(x1 * x1, axis=-1, keepdims=True)
            h2_ref[rows, :] = (x1 * lax.rsqrt(ms + EPS) * g2_ref[...]).astype(BF16)
            o_ref[rows, :] = x1

    a = jnp.maximum(_dot(h2_ref[...], w1_ref[...]), 0.0)
    o_ref[...] += _dot((a * a).astype(BF16), w2_ref[...])


def _out_mlp(x, yc, yd, yg, wo, g2, w1, w2, *, tm, tf, layer):
    t = x.shape[0]
    row = lambda i, f: (i, 0)
    fixed = lambda i, f: (0, 0)
    return pl.pallas_call(
        _out_mlp_kernel,
        grid=(t // tm, D_FF // tf),
        in_specs=[
            pl.BlockSpec((tm, D_MODEL), row),
            pl.BlockSpec((tm, CONV_W), row),
            pl.BlockSpec((tm, DIFF_W), row),
            pl.BlockSpec((tm, GLA_W), row),
            pl.BlockSpec((None, D_MODEL, D_MODEL), lambda i, f: (layer, 0, 0)),
            pl.BlockSpec((1, D_MODEL), fixed),
            pl.BlockSpec((None, D_MODEL, tf), lambda i, f: (layer, 0, f)),
            pl.BlockSpec((None, tf, D_MODEL), lambda i, f: (layer, f, 0)),
        ],
        out_specs=pl.BlockSpec((tm, D_MODEL), row),
        out_shape=jax.ShapeDtypeStruct((t, D_MODEL), F32),
        scratch_shapes=[pltpu.VMEM((tm, D_MODEL), BF16)],
        compiler_params=pltpu.CompilerParams(
            dimension_semantics=("arbitrary", "arbitrary"), vmem_limit_bytes=VMEM_LIMIT_BYTES),
        name="out_mlp",
    )(x, yc, yd, yg, wo, g2, w1, w2)


def _tile_sizes(seq):
    tk = min(ATTN_KEY_TILE, seq // 4)
    return dict(tk=tk, tm_in=2 * tk, tq=4 * tk, r_gla=min(GLA_ROWS, seq), tm_mlp=min(MLP_ROWS, seq),
                tf=MLP_HIDDEN_TILE)


def _block_diag_ones(n, blk):
    idx = jnp.arange(n) // blk
    return idx[:, None] == idx[None, :]


def _alibi_slopes2():
    return np.asarray([2.0 ** (-8.0 * (hd + 1) / DIFF_HEADS) * LOG2E for hd in range(DIFF_HEADS)],
                      np.float32)


def _position_columns(seq):
    width = DIFF_HEADS * AUG_W
    pos = lax.broadcasted_iota(jnp.int32, (seq, width), 0)
    col = lax.broadcasted_iota(jnp.int32, (seq, width), 1) % AUG_W
    c = jnp.asarray(np.repeat(_alibi_slopes2(), AUG_W))[None, :]
    c_hi = c.astype(BF16).astype(F32)
    c_lo = c - c_hi
    block = (pos - pos % LANES).astype(F32)
    rem = (pos % LANES).astype(F32)
    augq = jnp.where((col == 0) | (col == 2), c_hi,
                     jnp.where((col == 1) | (col == 3), c_lo,
                               jnp.where(col == 4, -c * pos.astype(F32), 0.0)))
    augk = jnp.where(col < 2, block, jnp.where(col < 4, rem, jnp.where(col == 4, 1.0, 0.0)))
    return augq.astype(BF16), augk.astype(BF16)


def kernel(x, ln1_g, w_in, conv_w, q_norm_g, k_norm_g, diff_lambda, diff_subln_g, gla_alpha_w,
           gla_alpha_b, gla_norm_g, w_out, ln2_g, w_mlp1, w_mlp2):
    batch, seq, d_model = x.shape
    depth = w_in.shape[0]
    assert d_model == D_MODEL and seq % (2 * CHUNK) == 0
    ts = _tile_sizes(seq)
    r = ts["r_gla"]

    bd_q = _block_diag_ones(DIFF_W, DIFF_DH).astype(BF16)
    bd_g = _block_diag_ones(GLA_W, GLA_DV).astype(BF16)
    same_chunk = _block_diag_ones(r, CHUNK)
    causal = jnp.arange(r)[:, None] >= jnp.arange(r)[None, :]
    lmat = (same_chunk & causal).astype(BF16)
    amask = (same_chunk & causal).astype(F32)
    augq, augk = _position_columns(seq)

    w_all = w_in.astype(BF16)
    w_a = jnp.pad(w_in[:, :, D_MAIN:], ((0, 0), (0, 0), (0, LANES - GLA_RANK))).astype(BF16)
    a_w = jnp.pad(gla_alpha_w, ((0, 0), (0, LANES - GLA_RANK), (0, 0))).astype(BF16)
    conv_w8 = jnp.pad(conv_w, ((0, 0), (0, SUBLANES - CONV_K), (0, 0)))
    w_o = w_out.astype(BF16)
    w_1 = w_mlp1.astype(BF16)
    w_2 = w_mlp2.astype(BF16)

    xf = x.reshape(batch * seq, D_MODEL)
    for l in range(depth):
        lam_init = 0.8 - 0.6 * math.exp(-0.3 * l)
        lp = diff_lambda[l].astype(F32)
        lam = jnp.exp(jnp.sum(lp[0] * lp[1])) - jnp.exp(jnp.sum(lp[2] * lp[3])) + lam_init
        bound2 = 1.02 * LOG2E * (DIFF_DH ** 0.5) * jnp.max(jnp.abs(q_norm_g[l])) * jnp.max(jnp.abs(k_norm_g[l]))
        bound_is_safe = (2.0 * bound2 <= MAX_EXP2_ARG).astype(F32)
        scal = jnp.concatenate([jnp.stack([lam, jnp.asarray(1.0 - lam_init, F32)]),
                                jnp.asarray(_alibi_slopes2()), bound_is_safe[None]]).astype(F32)
        qg = jnp.tile(q_norm_g[l] * (DIFF_DH ** -0.5), 2 * DIFF_HEADS)[None, :]
        kg = jnp.tile(k_norm_g[l], 2 * DIFF_HEADS)[None, :]

        yconv, qa, ka, vt, zg, za = _in_proj(
            xf, ln1_g[l][None, :], w_all, w_a[l], conv_w8[l], qg, kg, bd_q,
            augq, augk, seq=seq, tm=ts["tm_in"], g=ts["tk"], layer=l)
        ydiff = _diff_attn(scal, qa, ka, vt, diff_subln_g[l][:, None],
                           batch=batch, seq=seq, tq=ts["tq"])
        ygla = _gla(zg.reshape(batch, seq, 4 * GLA_W), za.reshape(batch, seq, LANES), a_w[l],
                    gla_alpha_b[l][None, :], jnp.tile(gla_norm_g[l], GLA_HEADS)[None, :],
                    lmat, amask, bd_g, batch=batch, seq=seq, r=r)
        xf = _out_mlp(xf, yconv, ydiff, ygla.reshape(batch * seq, GLA_W), w_o, ln2_g[l][None, :],
                      w_1, w_2, tm=ts["tm_mlp"], tf=ts["tf"], layer=l)
    return xf.reshape(batch, seq, D_MODEL)
```

```python
import functools
import math

import jax
import jax.numpy as jnp
import numpy as np
from jax import lax
from jax.experimental import pallas as pl
from jax.experimental.pallas import tpu as pltpu

F32 = jnp.float32
BF16 = jnp.bfloat16

D_MODEL = 1024
CHUNK = 64
EPS = 1e-6
CONV_W = 256
CONV_K = 3
DIFF_HEADS = 4
DIFF_DH = 64
DIFF_DV = 128
DIFF_W = DIFF_HEADS * DIFF_DV
GLA_HEADS = 4
GLA_DK = 64
GLA_DV = 64
GLA_W = GLA_HEADS * GLA_DV
GLA_RANK = 16
GLA_TAU = 16.0
D_FF = 4 * D_MODEL

_C_CONV = 0
_C_Q = 3 * CONV_W
_C_K = _C_Q + DIFF_HEADS * 2 * DIFF_DH
_C_V = _C_K + DIFF_HEADS * 2 * DIFF_DH
_C_G = _C_V + DIFF_W
_C_A = _C_G + 4 * GLA_W
D_MAIN = _C_A

LANES = 128
AUG_W = LANES
HEAD_W = 2 * DIFF_DH + AUG_W
LOG2E = math.log2(math.e)
MAX_EXP2_ARG = 60.0
GLA_SEQS_PER_STEP = 4
V_ROWS = DIFF_DV + 16
MASKED = -1e30
VMEM_LIMIT_BYTES = 52 * 1024 * 1024

SUBLANES = 8
ATTN_KEY_TILE = 512
GLA_ROWS = 256
MLP_ROWS = 1024
MLP_HIDDEN_TILE = 1024
MLP_SUBTILES = 4


def _nt_dot(a, b):
    return lax.dot_general(a, b, (((1,), (1,)), ((), ())), preferred_element_type=F32)


def _dot(a, b):
    return jnp.dot(a, b, preferred_element_type=F32)


def _in_proj_kernel(x_ref, g1_ref, w_ref, wa_ref, convw_ref, qg_ref, kg_ref, bd_ref,
                    augq_ref, augk_ref,
                    yconv_ref, qa_ref, ka_ref, vt_ref, zg_ref, za_ref, carry_ref,
                    *, tiles_per_seq):
    i = pl.program_id(0)
    g = vt_ref.shape[-1]
    parts = [slice(p * g, (p + 1) * g) for p in range(x_ref.shape[0] // g)]

    @pl.when(i % tiles_per_seq == 0)
    def _():
        carry_ref[...] = jnp.zeros_like(carry_ref)

    prev = carry_ref[...]
    row = lax.broadcasted_iota(jnp.int32, (g, CONV_W), 0)
    cw = convw_ref[...]
    h = []
    for p, rows in enumerate(parts):
        x = x_ref[rows, :]
        ms = jnp.mean(x * x, axis=-1, keepdims=True)
        h.append((x * lax.rsqrt(ms + EPS) * g1_ref[...]).astype(BF16))
        zc = _dot(h[p], w_ref[:, _C_CONV:_C_Q])
        u, cb, cc = zc[:, :CONV_W], zc[:, CONV_W:2 * CONV_W], zc[:, 2 * CONV_W:]
        z = cc * u
        last, before_last = prev[SUBLANES - 1:SUBLANES, :], prev[SUBLANES - 2:SUBLANES - 1, :]
        z1 = jnp.where(row == 0, last, pltpu.roll(z, 1, 0))
        z2 = jnp.where(row == 0, before_last, jnp.where(row == 1, last, pltpu.roll(z, 2, 0)))
        yconv_ref[rows, :] = (cb * (cw[0:1, :] * z2 + cw[1:2, :] * z1 + cw[2:3, :] * z)).astype(BF16)
        prev = z[g - SUBLANES:, :]
    carry_ref[...] = prev

    def seg_norm(zz, g_ref):
        ssq = _dot((zz * zz).astype(BF16), bd_ref[...])
        return zz * lax.rsqrt(ssq * (1.0 / DIFF_DH) + EPS) * g_ref[...]

    sub = lax.broadcasted_iota(jnp.int32, (V_ROWS - DIFF_DV, g), 0)
    ones_row = jnp.where(sub == 0, 1.0, 0.0).astype(BF16)
    for col0, g_ref, dst_ref, aug_ref in ((_C_Q, qg_ref, qa_ref, augq_ref), (_C_K, kg_ref, ka_ref, augk_ref)):
        for p, rows in enumerate(parts):
            zn = seg_norm(_dot(h[p], w_ref[:, col0:col0 + DIFF_W]), g_ref).astype(BF16)
            for hd in range(DIFF_HEADS):
                lo, mid, hi = hd * HEAD_W, hd * HEAD_W + 2 * DIFF_DH, (hd + 1) * HEAD_W
                dst_ref[rows, lo:mid] = zn[:, hd * 2 * DIFF_DH:(hd + 1) * 2 * DIFF_DH]
                dst_ref[rows, mid:hi] = aug_ref[rows, hd * AUG_W:(hd + 1) * AUG_W]
    for p, rows in enumerate(parts):
        zv = _dot(h[p], w_ref[:, _C_V:_C_G])
        for hd in range(DIFF_HEADS):
            vt_ref[hd, p, 0:DIFF_DV, :] = zv[:, hd * DIFF_DV:(hd + 1) * DIFF_DV].T.astype(BF16)
            vt_ref[hd, p, DIFF_DV:, :] = ones_row

    for p, rows in enumerate(parts):
        za_ref[rows, :] = _dot(h[p], wa_ref[...]).astype(BF16)
        zg_ref[rows, :] = _dot(h[p], w_ref[:, _C_G:_C_A]).astype(BF16)


def _in_proj(x, g1, w, wa, convw, qg, kg, bd, augq, augk, *, seq, tm, g, layer):
    t = x.shape[0]
    tps = seq // tm
    batch = t // seq
    row = lambda i: (i, 0)
    fixed = lambda i: (0, 0)
    pos = lambda i: (i % tps, 0)
    out_w = DIFF_HEADS * HEAD_W
    return pl.pallas_call(
        functools.partial(_in_proj_kernel, tiles_per_seq=tps),
        grid=(t // tm,),
        in_specs=[
            pl.BlockSpec((tm, D_MODEL), row),
            pl.BlockSpec((1, D_MODEL), fixed),
            pl.BlockSpec((None, D_MODEL, D_MAIN), lambda i: (layer, 0, 0)),
            pl.BlockSpec((D_MODEL, LANES), fixed),
            pl.BlockSpec((SUBLANES, CONV_W), fixed),
            pl.BlockSpec((1, DIFF_W), fixed),
            pl.BlockSpec((1, DIFF_W), fixed),
            pl.BlockSpec((DIFF_W, DIFF_W), fixed),
            pl.BlockSpec((tm, DIFF_HEADS * AUG_W), pos),
            pl.BlockSpec((tm, DIFF_HEADS * AUG_W), pos),
        ],
        out_specs=[
            pl.BlockSpec((tm, CONV_W), row),
            pl.BlockSpec((tm, out_w), row),
            pl.BlockSpec((tm, out_w), row),
            pl.BlockSpec((None, DIFF_HEADS, tm // g, V_ROWS, g), lambda i: (i // tps, 0, i % tps, 0, 0)),
            pl.BlockSpec((tm, 4 * GLA_W), row),
            pl.BlockSpec((tm, LANES), row),
        ],
        out_shape=[
            jax.ShapeDtypeStruct((t, CONV_W), BF16),
            jax.ShapeDtypeStruct((t, out_w), BF16),
            jax.ShapeDtypeStruct((t, out_w), BF16),
            jax.ShapeDtypeStruct((batch, DIFF_HEADS, seq // g, V_ROWS, g), BF16),
            jax.ShapeDtypeStruct((t, 4 * GLA_W), BF16),
            jax.ShapeDtypeStruct((t, LANES), BF16),
        ],
        scratch_shapes=[pltpu.VMEM((SUBLANES, CONV_W), F32)],
        compiler_params=pltpu.CompilerParams(
            dimension_semantics=("arbitrary",), vmem_limit_bytes=VMEM_LIMIT_BYTES),
        name="in_proj",
    )(x, g1, w, wa, convw, qg, kg, bd, augq, augk)


def _diff_attn_kernel(scal_ref, q_ref, k_ref, vt_ref, subg_ref, o_ref,
                      acc0_ref, acc1_ref, m0_ref, m1_ref, *s_refs):
    hd = pl.program_id(1)
    qi = pl.program_id(2)
    tq = q_ref.shape[0]
    g = vt_ref.shape[-1]
    lam = scal_ref[0]
    out_scale = scal_ref[1]
    slope2 = scal_ref[2 + hd]

    accs = (acc0_ref, acc1_ref)
    ms = (m0_ref, m1_ref)

    def transposed_queries():
        qt = q_ref[...].astype(F32).T
        feat = lax.broadcasted_iota(jnp.int32, qt.shape, 0)
        qt = jnp.where(feat < 2 * DIFF_DH, qt * LOG2E, qt)
        qt0 = jnp.where((feat >= DIFF_DH) & (feat < 2 * DIFF_DH), 0.0, qt).astype(BF16)
        qt1 = jnp.where(feat < DIFF_DH, 0.0, qt).astype(BF16)
        return qt0, qt1

    def key_tiles(jobs, init, track_max, qt0, qt1):
        m_cur = [None if init or not track_max else m_ref[...] for m_ref in ms]
        staged = []
        for t, (gi, q_lo, corr) in enumerate(jobs):
            kt = k_ref[pl.ds(pl.multiple_of(gi * g, g), g), :]
            for c, qtc in enumerate((qt0, qt1)):
                s_ref = s_refs[2 * t + c]
                s = _dot(kt, qtc[:, q_lo:])
                s_d = s[:, :g] + corr
                s_ref[:, q_lo:q_lo + g] = s_d
                if q_lo + g < tq:
                    s_ref[:, q_lo + g:] = s[:, g:]
                first = init and t == 0
                if not track_max:
                    staged.append((gi, q_lo, s_ref, accs[c], None, None, first))
                    continue
                m_tile = jnp.max(s_d, axis=0, keepdims=True)
                if q_lo + g < tq:
                    m_tile = jnp.concatenate([m_tile, jnp.max(s[:, g:], axis=0, keepdims=True)], axis=1)
                if m_cur[c] is None:
                    m_new, alpha = m_tile, None
                else:
                    m_new = jnp.maximum(m_cur[c][:, q_lo:], m_tile)
                    alpha = jnp.exp2(m_cur[c][:, q_lo:] - m_new)
                    if q_lo:
                        m_new_full = jnp.concatenate([m_cur[c][:, :q_lo], m_new], axis=1)
                    else:
                        m_new_full = m_new
                staged.append((gi, q_lo, s_ref, accs[c], m_new, alpha, first))
                m_cur[c] = m_new if alpha is None else m_new_full
        for gi, q_lo, s_ref, acc, m_new, alpha, first in staged:
            s = s_ref[:, q_lo:] if m_new is None else s_ref[:, q_lo:] - m_new
            pv = _dot(vt_ref[gi], jnp.exp2(s).astype(BF16))
            if first:
                acc[...] = pv
            elif alpha is None:
                acc[:, q_lo:] += pv
            else:
                acc[:, q_lo:] = acc[:, q_lo:] * alpha + pv
        if track_max:
            for c, m_ref in enumerate(ms):
                m_ref[...] = m_cur[c]

    nsub = tq // g
    key = lax.broadcasted_iota(jnp.int32, (g, g), 0)
    qry = lax.broadcasted_iota(jnp.int32, (g, g), 1)
    ahead = key - qry
    visible = (key // CHUNK) <= (qry // CHUNK)
    corr = jnp.where(ahead <= 0, 0.0,
                     jnp.where(visible, (-2.0 * slope2) * ahead.astype(F32), MASKED))
    def make_body(track_max, qt0, qt1):
        def body(jj, carry):
            kt = k_ref[pl.ds(pl.multiple_of(2 * jj * g, 2 * g), 2 * g), :]
            state = []
            for c, qtc in enumerate((qt0, qt1)):
                s = _dot(kt, qtc)
                s_refs[c][...] = s[:g, :]
                s_refs[2 + c][...] = s[g:, :]
                if track_max:
                    m_old = ms[c][...]
                    m_new = jnp.maximum(m_old, jnp.max(s, axis=0, keepdims=True))
                    state.append((m_new, jnp.exp2(m_old - m_new)))
            for c, acc in enumerate(accs):
                s_a, s_b = s_refs[c][...], s_refs[2 + c][...]
                if track_max:
                    m_new, alpha = state[c]
                    s_a, s_b = s_a - m_new, s_b - m_new
                pv = (_dot(vt_ref[2 * jj], jnp.exp2(s_a).astype(BF16))
                      + _dot(vt_ref[2 * jj + 1], jnp.exp2(s_b).astype(BF16)))
                if track_max:
                    acc[...] = acc[...] * alpha + pv
                    ms[c][...] = m_new
                else:
                    acc[...] += pv
            return carry
        return body

    def all_tiles(track_max):
        qt0, qt1 = transposed_queries()
        for a in range(0, nsub, 2):
            key_tiles(((nsub * qi + a, a * g, corr), (nsub * qi + a + 1, (a + 1) * g, corr)),
                      a == 0, track_max, qt0, qt1)
        lax.fori_loop(0, qi * (nsub // 2), make_body(track_max, qt0, qt1), 0)

    bound_is_safe = scal_ref[2 + DIFF_HEADS] > 0.5

    @pl.when(bound_is_safe)
    def _():
        all_tiles(False)

    @pl.when(jnp.logical_not(bound_is_safe))
    def _():
        all_tiles(True)

    a0 = acc0_ref[...]
    a1 = acc1_ref[...]
    ot = (a0[:DIFF_DV, :] / a0[DIFF_DV:DIFF_DV + 1, :]
          - lam * (a1[:DIFF_DV, :] / a1[DIFF_DV:DIFF_DV + 1, :]))
    mean_sq = jnp.mean(ot * ot, axis=0, keepdims=True)
    yt = (ot * lax.rsqrt(mean_sq + EPS) * subg_ref[...]) * out_scale
    o_ref[...] = yt.T.astype(BF16)


def _diff_attn(scal, qa, ka, vt, subg, *, batch, seq, tq):
    nq = seq // tq
    g = vt.shape[-1]
    assert tq % (2 * g) == 0
    return pl.pallas_call(
        _diff_attn_kernel,
        grid=(batch, DIFF_HEADS, nq),
        in_specs=[
            pl.BlockSpec(memory_space=pltpu.SMEM),
            pl.BlockSpec((tq, HEAD_W), lambda b, h, i: (b * nq + i, h)),
            pl.BlockSpec((seq, HEAD_W), lambda b, h, i: (b, h)),
            pl.BlockSpec((None, None, seq // g, V_ROWS, g), lambda b, h, i: (b, h, 0, 0, 0)),
            pl.BlockSpec((DIFF_DV, 1), lambda b, h, i: (0, 0)),
        ],
        out_specs=pl.BlockSpec((tq, DIFF_DV), lambda b, h, i: (b * nq + i, h)),
        out_shape=jax.ShapeDtypeStruct((batch * seq, DIFF_W), BF16),
        scratch_shapes=([pltpu.VMEM((V_ROWS, tq), F32)] * 2 + [pltpu.VMEM((1, tq), F32)] * 2
                        + [pltpu.VMEM((g, tq), F32)] * 4),
        compiler_params=pltpu.CompilerParams(
            dimension_semantics=("arbitrary", "arbitrary", "arbitrary"),
            vmem_limit_bytes=VMEM_LIMIT_BYTES),
        name="diff_attn",
    )(scal, qa, ka, vt, subg)


def _gla_kernel(zg_ref, za_ref, aw_ref, ab_ref, ng_ref, lmat_ref, amask_ref, bd_ref,
                o_ref, st_ref):
    @pl.when(pl.program_id(1) == 0)
    def _():
        st_ref[...] = jnp.zeros_like(st_ref)

    nb, r = zg_ref.shape[0], zg_ref.shape[1]
    seqs = range(nb)
    lane = lax.broadcasted_iota(jnp.int32, (1, GLA_W), 1)
    rowid = lax.broadcasted_iota(jnp.int32, (r, 1), 0)
    bd = bd_ref[...]
    bd_f = bd.astype(F32)

    log_a = []
    for bb in seqs:
        pre = _dot(za_ref[bb], aw_ref[...]) + ab_ref[...]
        log_a.append((jnp.minimum(pre, 0.0) - jnp.log(1.0 + jnp.exp(-jnp.abs(pre)))) * (1.0 / GLA_TAU))
    q_in, k_in, k_end, dec = [], [], [], []
    for bb in seqs:
        hi = log_a[bb].astype(BF16)
        lo = (log_a[bb] - hi.astype(F32)).astype(BF16)
        cs = _dot(lmat_ref[...], jnp.concatenate([hi, lo], axis=1))
        b = cs[:, :GLA_W] + cs[:, GLA_W:]
        b_last = [b[(c + 1) * CHUNK - 1:(c + 1) * CHUNK, :] for c in range(r // CHUNK)]
        b_tot = jnp.concatenate([jnp.broadcast_to(bl, (CHUNK, GLA_W)) for bl in b_last], axis=0)
        gq = zg_ref[bb, :, 0:GLA_W].astype(F32)
        gk = zg_ref[bb, :, GLA_W:2 * GLA_W].astype(F32)
        q_in.append(((gq * (GLA_DK ** -0.5)) * jnp.exp(b)).astype(BF16))
        k_in.append((gk * jnp.exp(-b)).astype(BF16))
        k_end.append(gk * jnp.exp(b_tot - b))
        dec.append([jnp.exp(bl) for bl in b_last])

    o = [jnp.zeros((r, GLA_W), F32) for _ in seqs]
    for hd in range(GLA_HEADS):
        in_head = (lane >= hd * GLA_DK) & (lane < (hd + 1) * GLA_DK)
        for bb in seqs:
            att = _nt_dot(jnp.where(in_head, q_in[bb], jnp.zeros_like(q_in[bb])), k_in[bb])
            att = jnp.where(amask_ref[...] > 0.0, att, 0.0).astype(BF16)
            o[bb] = jnp.where(in_head, _dot(att, zg_ref[bb, :, 2 * GLA_W:3 * GLA_W]), o[bb])

    kv_t = []
    for bb in seqs:
        gv_t = zg_ref[bb, :, 2 * GLA_W:3 * GLA_W].astype(F32).T.astype(BF16)
        per_chunk = []
        for c in range(r // CHUNK):
            in_chunk = (rowid >= c * CHUNK) & (rowid < (c + 1) * CHUNK)
            per_chunk.append(_dot(gv_t, jnp.where(in_chunk, k_end[bb], 0.0).astype(BF16)) * bd_f)
        kv_t.append(per_chunk)
    st = [st_ref[bb] for bb in seqs]
    inter = [[] for _ in seqs]
    for c in range(r // CHUNK):
        rows = slice(c * CHUNK, (c + 1) * CHUNK)
        for bb in seqs:
            inter[bb].append(_nt_dot(q_in[bb][rows, :], st[bb].astype(BF16)))
            st[bb] = st[bb] * dec[bb][c] + kv_t[bb][c]
    for bb in seqs:
        st_ref[bb] = st[bb]

    for bb in seqs:
        ob = o[bb] + jnp.concatenate(inter[bb], axis=0)
        ssq = _dot((ob * ob).astype(BF16), bd)
        on = ob * lax.rsqrt(ssq * (1.0 / GLA_DV) + EPS) * ng_ref[...]
        gg = zg_ref[bb, :, 3 * GLA_W:4 * GLA_W].astype(F32)
        o_ref[bb] = (on * (gg * (1.0 / (1.0 + jnp.exp(-gg))))).astype(BF16)


def _gla(zg, za, aw, ab, ng, lmat, amask, bd, *, batch, seq, r):
    nb = min(GLA_SEQS_PER_STEP, batch)
    row = lambda bg, t: (bg, t, 0)
    fixed = lambda bg, t: (0, 0)
    return pl.pallas_call(
        _gla_kernel,
        grid=(batch // nb, seq // r),
        in_specs=[
            pl.BlockSpec((nb, r, 4 * GLA_W), row),
            pl.BlockSpec((nb, r, LANES), row),
            pl.BlockSpec((LANES, GLA_W), fixed),
            pl.BlockSpec((1, GLA_W), fixed),
            pl.BlockSpec((1, GLA_W), fixed),
            pl.BlockSpec((r, r), fixed),
            pl.BlockSpec((r, r), fixed),
            pl.BlockSpec((GLA_W, GLA_W), fixed),
        ],
        out_specs=pl.BlockSpec((nb, r, GLA_W), row),
        out_shape=jax.ShapeDtypeStruct((batch, seq, GLA_W), BF16),
        scratch_shapes=[pltpu.VMEM((nb, GLA_W, GLA_W), F32)],
        compiler_params=pltpu.CompilerParams(
            dimension_semantics=("arbitrary", "arbitrary"), vmem_limit_bytes=VMEM_LIMIT_BYTES),
        name="gla",
    )(zg, za, aw, ab, ng, lmat, amask, bd)


def _out_mlp_kernel(x_ref, yc_ref, yd_ref, yg_ref, wo_ref, g2_ref, w1_ref, w2_ref,
                    o_ref, h2_ref):
    f = pl.program_id(1)

    @pl.when(f == 0)
    def _():
        sub = x_ref.shape[0] // MLP_SUBTILES
        for p in range(MLP_SUBTILES):
            rows = slice(p * sub, (p + 1) * sub)
            y = (_dot(yc_ref[rows, :], wo_ref[0:CONV_W, :])
                 + _dot(yd_ref[rows, :], wo_ref[CONV_W:CONV_W + DIFF_W, :])
                 + _dot(yg_ref[rows, :], wo_ref[CONV_W + DIFF_W:, :]))
            x1 = x_ref[rows, :] + y
            ms = jnp.mean(x1 * x1, axis=-1, keepdims=True)
            h2_ref[rows, :] = (x1 * lax.rsqrt(ms + EPS) * g2_ref[...]).astype(BF16)
            o_ref[rows, :] = x1

    a = jnp.maximum(_dot(h2_ref[...], w1_ref[...]), 0.0)
    o_ref[...] += _dot((a * a).astype(BF16), w2_ref[...])


def _out_mlp(x, yc, yd, yg, wo, g2, w1, w2, *, tm, tf, layer):
    t = x.shape[0]
    row = lambda i, f: (i, 0)
    fixed = lambda i, f: (0, 0)
    return pl.pallas_call(
        _out_mlp_kernel,
        grid=(t // tm, D_FF // tf),
        in_specs=[
            pl.BlockSpec((tm, D_MODEL), row),
            pl.BlockSpec((tm, CONV_W), row),
            pl.BlockSpec((tm, DIFF_W), row),
            pl.BlockSpec((tm, GLA_W), row),
            pl.BlockSpec((None, D_MODEL, D_MODEL), lambda i, f: (layer, 0, 0)),
            pl.BlockSpec((1, D_MODEL), fixed),
            pl.BlockSpec((None, D_MODEL, tf), lambda i, f: (layer, 0, f)),
            pl.BlockSpec((None, tf, D_MODEL), lambda i, f: (layer, f, 0)),
        ],
        out_specs=pl.BlockSpec((tm, D_MODEL), row),
        out_shape=jax.ShapeDtypeStruct((t, D_MODEL), F32),
        scratch_shapes=[pltpu.VMEM((tm, D_MODEL), BF16)],
        compiler_params=pltpu.CompilerParams(
            dimension_semantics=("arbitrary", "arbitrary"), vmem_limit_bytes=VMEM_LIMIT_BYTES),
        name="out_mlp",
    )(x, yc, yd, yg, wo, g2, w1, w2)


def _tile_sizes(seq):
    tk = min(ATTN_KEY_TILE, seq // 4)
    return dict(tk=tk, tm_in=2 * tk, tq=4 * tk, r_gla=min(GLA_ROWS, seq), tm_mlp=min(MLP_ROWS, seq),
                tf=MLP_HIDDEN_TILE)


def _block_diag_ones(n, blk):
    idx = jnp.arange(n) // blk
    return idx[:, None] == idx[None, :]


def _alibi_slopes2():
    return np.asarray([2.0 ** (-8.0 * (hd + 1) / DIFF_HEADS) * LOG2E for hd in range(DIFF_HEADS)],
                      np.float32)


def _position_columns(seq):
    width = DIFF_HEADS * AUG_W
    pos = lax.broadcasted_iota(jnp.int32, (seq, width), 0)
    col = lax.broadcasted_iota(jnp.int32, (seq, width), 1) % AUG_W
    c = jnp.asarray(np.repeat(_alibi_slopes2(), AUG_W))[None, :]
    c_hi = c.astype(BF16).astype(F32)
    c_lo = c - c_hi
    block = (pos - pos % LANES).astype(F32)
    rem = (pos % LANES).astype(F32)
    augq = jnp.where((col == 0) | (col == 2), c_hi,
                     jnp.where((col == 1) | (col == 3), c_lo,
                               jnp.where(col == 4, -c * pos.astype(F32), 0.0)))
    augk = jnp.where(col < 2, block, jnp.where(col < 4, rem, jnp.where(col == 4, 1.0, 0.0)))
    return augq.astype(BF16), augk.astype(BF16)


def kernel(x, ln1_g, w_in, conv_w, q_norm_g, k_norm_g, diff_lambda, diff_subln_g, gla_alpha_w,
           gla_alpha_b, gla_norm_g, w_out, ln2_g, w_mlp1, w_mlp2):
    batch, seq, d_model = x.shape
    depth = w_in.shape[0]
    assert d_model == D_MODEL and seq % (2 * CHUNK) == 0
    ts = _tile_sizes(seq)
    r = ts["r_gla"]

    bd_q = _block_diag_ones(DIFF_W, DIFF_DH).astype(BF16)
    bd_g = _block_diag_ones(GLA_W, GLA_DV).astype(BF16)
    same_chunk = _block_diag_ones(r, CHUNK)
    causal = jnp.arange(r)[:, None] >= jnp.arange(r)[None, :]
    lmat = (same_chunk & causal).astype(BF16)
    amask = (same_chunk & causal).astype(F32)
    augq, augk = _position_columns(seq)

    w_all = w_in.astype(BF16)
    w_a = jnp.pad(w_in[:, :, D_MAIN:], ((0, 0), (0, 0), (0, LANES - GLA_RANK))).astype(BF16)
    a_w = jnp.pad(gla_alpha_w, ((0, 0), (0, LANES - GLA_RANK), (0, 0))).astype(BF16)
    conv_w8 = jnp.pad(conv_w, ((0, 0), (0, SUBLANES - CONV_K), (0, 0)))
    w_o = w_out.astype(BF16)
    w_1 = w_mlp1.astype(BF16)
    w_2 = w_mlp2.astype(BF16)

    xf = x.reshape(batch * seq, D_MODEL)
    for l in range(depth):
        lam_init = 0.8 - 0.6 * math.exp(-0.3 * l)
        lp = diff_lambda[l].astype(F32)
        lam = jnp.exp(jnp.sum(lp[0] * lp[1])) - jnp.exp(jnp.sum(lp[2] * lp[3])) + lam_init
        bound2 = (1.02 * LOG2E * (DIFF_DH ** 0.5) * jnp.max(jnp.abs(q_norm_g[l])) * jnp.max(jnp.abs(k_norm_g[l]))
                  + float(_alibi_slopes2()[0]) * seq * 2.0 ** -9)
        bound_is_safe = (bound2 <= MAX_EXP2_ARG).astype(F32)
        scal = jnp.concatenate([jnp.stack([lam, jnp.asarray(1.0 - lam_init, F32)]),
                                jnp.asarray(_alibi_slopes2()), bound_is_safe[None]]).astype(F32)
        qg = jnp.tile(q_norm_g[l] * (DIFF_DH ** -0.5), 2 * DIFF_HEADS)[None, :]
        kg = jnp.tile(k_norm_g[l], 2 * DIFF_HEADS)[None, :]

        yconv, qa, ka, vt, zg, za = _in_proj(
            xf, ln1_g[l][None, :], w_all, w_a[l], conv_w8[l], qg, kg, bd_q,
            augq, augk, seq=seq, tm=ts["tm_in"], g=ts["tk"], layer=l)
        ydiff = _diff_attn(scal, qa, ka, vt, diff_subln_g[l][:, None],
                           batch=batch, seq=seq, tq=ts["tq"])
        ygla = _gla(zg.reshape(batch, seq, 4 * GLA_W), za.reshape(batch, seq, LANES), a_w[l],
                    gla_alpha_b[l][None, :], jnp.tile(gla_norm_g[l], GLA_HEADS)[None, :],
                    lmat, amask, bd_g, batch=batch, seq=seq, r=r)
        xf = _out_mlp(xf, yconv, ydiff, ygla.reshape(batch * seq, GLA_W), w_o, ln2_g[l][None, :],
                      w_1, w_2, tm=ts["tm_mlp"], tf=ts["tf"], layer=l)
    return xf.reshape(batch, seq, D_MODEL)
```

```python
import functools
import math

import jax
import jax.numpy as jnp
import numpy as np
from jax import lax
from jax.experimental import pallas as pl
from jax.experimental.pallas import tpu as pltpu

F32 = jnp.float32
BF16 = jnp.bfloat16

D_MODEL = 1024
CHUNK = 64
EPS = 1e-6
CONV_W = 256
CONV_K = 3
DIFF_HEADS = 4
DIFF_DH = 64
DIFF_DV = 128
DIFF_W = DIFF_HEADS * DIFF_DV
GLA_HEADS = 4
GLA_DK = 64
GLA_DV = 64
GLA_W = GLA_HEADS * GLA_DV
GLA_RANK = 16
GLA_TAU = 16.0
D_FF = 4 * D_MODEL

_C_CONV = 0
_C_Q = 3 * CONV_W
_C_K = _C_Q + DIFF_HEADS * 2 * DIFF_DH
_C_V = _C_K + DIFF_HEADS * 2 * DIFF_DH
_C_G = _C_V + DIFF_W
_C_A = _C_G + 4 * GLA_W
D_MAIN = _C_A

LANES = 128
AUG_W = LANES
HEAD_W = 2 * DIFF_DH + AUG_W
LOG2E = math.log2(math.e)
MAX_EXP2_ARG = 60.0
GLA_SEQS_PER_STEP = 4
V_ROWS = DIFF_DV + 16
MASKED = -1e30
VMEM_LIMIT_BYTES = 52 * 1024 * 1024

SUBLANES = 8
ATTN_KEY_TILE = 512
GLA_ROWS = 256
MLP_ROWS = 1024
MLP_HIDDEN_TILE = 1024
MLP_SUBTILES = 4


def _nt_dot(a, b):
    return lax.dot_general(a, b, (((1,), (1,)), ((), ())), preferred_element_type=F32)


def _dot(a, b):
    return jnp.dot(a, b, preferred_element_type=F32)


def _in_proj_kernel(x_ref, g1_ref, w_ref, wa_ref, convw_ref, qg_ref, kg_ref, bd_ref,
                    augq_ref, augk_ref,
                    yconv_ref, qa_ref, ka_ref, vt_ref, zg_ref, za_ref, carry_ref,
                    *, tiles_per_seq):
    i = pl.program_id(0)
    g = vt_ref.shape[-1]
    parts = [slice(p * g, (p + 1) * g) for p in range(x_ref.shape[0] // g)]

    @pl.when(i % tiles_per_seq == 0)
    def _():
        carry_ref[...] = jnp.zeros_like(carry_ref)

    prev = carry_ref[...]
    row = lax.broadcasted_iota(jnp.int32, (g, CONV_W), 0)
    cw = convw_ref[...]
    h = []
    for p, rows in enumerate(parts):
        x = x_ref[rows, :]
        ms = jnp.mean(x * x, axis=-1, keepdims=True)
        h.append((x * lax.rsqrt(ms + EPS) * g1_ref[...]).astype(BF16))
        zc = _dot(h[p], w_ref[:, _C_CONV:_C_Q])
        u, cb, cc = zc[:, :CONV_W], zc[:, CONV_W:2 * CONV_W], zc[:, 2 * CONV_W:]
        z = cc * u
        last, before_last = prev[SUBLANES - 1:SUBLANES, :], prev[SUBLANES - 2:SUBLANES - 1, :]
        z1 = jnp.where(row == 0, last, pltpu.roll(z, 1, 0))
        z2 = jnp.where(row == 0, before_last, jnp.where(row == 1, last, pltpu.roll(z, 2, 0)))
        yconv_ref[rows, :] = (cb * (cw[0:1, :] * z2 + cw[1:2, :] * z1 + cw[2:3, :] * z)).astype(BF16)
        prev = z[g - SUBLANES:, :]
    carry_ref[...] = prev

    def seg_norm(zz, g_ref):
        ssq = _dot((zz * zz).astype(BF16), bd_ref[...])
        return zz * lax.rsqrt(ssq * (1.0 / DIFF_DH) + EPS) * g_ref[...]

    sub = lax.broadcasted_iota(jnp.int32, (V_ROWS - DIFF_DV, g), 0)
    ones_row = jnp.where(sub == 0, 1.0, 0.0).astype(BF16)
    for col0, g_ref, dst_ref, aug_ref in ((_C_Q, qg_ref, qa_ref, augq_ref), (_C_K, kg_ref, ka_ref, augk_ref)):
        for p, rows in enumerate(parts):
            zn = seg_norm(_dot(h[p], w_ref[:, col0:col0 + DIFF_W]), g_ref).astype(BF16)
            for hd in range(DIFF_HEADS):
                lo, mid, hi = hd * HEAD_W, hd * HEAD_W + 2 * DIFF_DH, (hd + 1) * HEAD_W
                dst_ref[rows, lo:mid] = zn[:, hd * 2 * DIFF_DH:(hd + 1) * 2 * DIFF_DH]
                dst_ref[rows, mid:hi] = aug_ref[rows, hd * AUG_W:(hd + 1) * AUG_W]
    for p, rows in enumerate(parts):
        zv = _dot(h[p], w_ref[:, _C_V:_C_G])
        for hd in range(DIFF_HEADS):
            vt_ref[hd, p, 0:DIFF_DV, :] = zv[:, hd * DIFF_DV:(hd + 1) * DIFF_DV].T.astype(BF16)
            vt_ref[hd, p, DIFF_DV:, :] = ones_row

    for p, rows in enumerate(parts):
        za_ref[rows, :] = _dot(h[p], wa_ref[...]).astype(BF16)
        zg_ref[rows, :] = _dot(h[p], w_ref[:, _C_G:_C_A]).astype(BF16)


def _in_proj(x, g1, w, wa, convw, qg, kg, bd, augq, augk, *, seq, tm, g, layer):
    t = x.shape[0]
    tps = seq // tm
    batch = t // seq
    row = lambda i: (i, 0)
    fixed = lambda i: (0, 0)
    pos = lambda i: (i % tps, 0)
    out_w = DIFF_HEADS * HEAD_W
    return pl.pallas_call(
        functools.partial(_in_proj_kernel, tiles_per_seq=tps),
        grid=(t // tm,),
        in_specs=[
            pl.BlockSpec((tm, D_MODEL), row),
            pl.BlockSpec((1, D_MODEL), fixed),
            pl.BlockSpec((None, D_MODEL, D_MAIN), lambda i: (layer, 0, 0)),
            pl.BlockSpec((D_MODEL, LANES), fixed),
            pl.BlockSpec((SUBLANES, CONV_W), fixed),
            pl.BlockSpec((1, DIFF_W), fixed),
            pl.BlockSpec((1, DIFF_W), fixed),
            pl.BlockSpec((DIFF_W, DIFF_W), fixed),
            pl.BlockSpec((tm, DIFF_HEADS * AUG_W), pos),
            pl.BlockSpec((tm, DIFF_HEADS * AUG_W), pos),
        ],
        out_specs=[
            pl.BlockSpec((tm, CONV_W), row),
            pl.BlockSpec((tm, out_w), row),
            pl.BlockSpec((tm, out_w), row),
            pl.BlockSpec((None, DIFF_HEADS, tm // g, V_ROWS, g), lambda i: (i // tps, 0, i % tps, 0, 0)),
            pl.BlockSpec((tm, 4 * GLA_W), row),
            pl.BlockSpec((tm, LANES), row),
        ],
        out_shape=[
            jax.ShapeDtypeStruct((t, CONV_W), BF16),
            jax.ShapeDtypeStruct((t, out_w), BF16),
            jax.ShapeDtypeStruct((t, out_w), BF16),
            jax.ShapeDtypeStruct((batch, DIFF_HEADS, seq // g, V_ROWS, g), BF16),
            jax.ShapeDtypeStruct((t, 4 * GLA_W), BF16),
            jax.ShapeDtypeStruct((t, LANES), BF16),
        ],
        scratch_shapes=[pltpu.VMEM((SUBLANES, CONV_W), F32)],
        compiler_params=pltpu.CompilerParams(
            dimension_semantics=("arbitrary",), vmem_limit_bytes=VMEM_LIMIT_BYTES),
        name="in_proj",
    )(x, g1, w, wa, convw, qg, kg, bd, augq, augk)


def _diff_attn_kernel(scal_ref, q_ref, k_ref, vt_ref, subg_ref, o_ref,
                      acc0_ref, acc1_ref, m0_ref, m1_ref, *s_refs):
    hd = pl.program_id(1)
    qi = pl.program_id(2)
    tq = q_ref.shape[0]
    g = vt_ref.shape[-1]
    lam = scal_ref[0]
    out_scale = scal_ref[1]
    slope2 = scal_ref[2 + hd]

    accs = (acc0_ref, acc1_ref)
    ms = (m0_ref, m1_ref)

    def transposed_queries():
        qt = q_ref[...].astype(F32).T
        feat = lax.broadcasted_iota(jnp.int32, qt.shape, 0)
        qt = jnp.where(feat < 2 * DIFF_DH, qt * LOG2E, qt)
        qt0 = jnp.where((feat >= DIFF_DH) & (feat < 2 * DIFF_DH), 0.0, qt).astype(BF16)
        qt1 = jnp.where(feat < DIFF_DH, 0.0, qt).astype(BF16)
        return qt0, qt1

    def key_tiles(jobs, init, track_max, qt0, qt1):
        m_cur = [None if init or not track_max else m_ref[...] for m_ref in ms]
        staged = []
        for t, (gi, q_lo, corr) in enumerate(jobs):
            kt = k_ref[pl.ds(pl.multiple_of(gi * g, g), g), :]
            for c, qtc in enumerate((qt0, qt1)):
                s_ref = s_refs[2 * t + c]
                s = _dot(kt, qtc[:, q_lo:])
                s_d = s[:, :g] + corr
                s_ref[:, q_lo:q_lo + g] = s_d
                if q_lo + g < tq:
                    s_ref[:, q_lo + g:] = s[:, g:]
                first = init and t == 0
                if not track_max:
                    staged.append((gi, q_lo, s_ref, accs[c], None, None, first))
                    continue
                m_tile = jnp.max(s_d, axis=0, keepdims=True)
                if q_lo + g < tq:
                    m_tile = jnp.concatenate([m_tile, jnp.max(s[:, g:], axis=0, keepdims=True)], axis=1)
                if m_cur[c] is None:
                    m_new, alpha = m_tile, None
                else:
                    m_new = jnp.maximum(m_cur[c][:, q_lo:], m_tile)
                    alpha = jnp.exp2(m_cur[c][:, q_lo:] - m_new)
                    if q_lo:
                        m_new_full = jnp.concatenate([m_cur[c][:, :q_lo], m_new], axis=1)
                    else:
                        m_new_full = m_new
                staged.append((gi, q_lo, s_ref, accs[c], m_new, alpha, first))
                m_cur[c] = m_new if alpha is None else m_new_full
        for gi, q_lo, s_ref, acc, m_new, alpha, first in staged:
            s = s_ref[:, q_lo:] if m_new is None else s_ref[:, q_lo:] - m_new
            pv = _dot(vt_ref[gi], jnp.exp2(s).astype(BF16))
            if first:
                acc[...] = pv
            elif alpha is None:
                acc[:, q_lo:] += pv
            else:
                acc[:, q_lo:] = acc[:, q_lo:] * alpha + pv
        if track_max:
            for c, m_ref in enumerate(ms):
                m_ref[...] = m_cur[c]

    nsub = tq // g
    def diagonal_correction():
        key = lax.broadcasted_iota(jnp.int32, (g, g), 0)
        qry = lax.broadcasted_iota(jnp.int32, (g, g), 1)
        ahead = key - qry
        visible = (key // CHUNK) <= (qry // CHUNK)
        return jnp.where(ahead <= 0, 0.0,
                         jnp.where(visible, (-2.0 * slope2) * ahead.astype(F32), MASKED))

    def make_body(track_max, qt0, qt1):
        def body(jj, carry):
            kt = k_ref[pl.ds(pl.multiple_of(2 * jj * g, 2 * g), 2 * g), :]
            state = []
            for c, qtc in enumerate((qt0, qt1)):
                s = _dot(kt, qtc)
                s_refs[c][...] = s[:g, :]
                s_refs[2 + c][...] = s[g:, :]
                if track_max:
                    m_old = ms[c][...]
                    m_new = jnp.maximum(m_old, jnp.max(s, axis=0, keepdims=True))
                    state.append((m_new, jnp.exp2(m_old - m_new)))
            for c, acc in enumerate(accs):
                s_a, s_b = s_refs[c][...], s_refs[2 + c][...]
                if track_max:
                    m_new, alpha = state[c]
                    s_a, s_b = s_a - m_new, s_b - m_new
                pv = (_dot(vt_ref[2 * jj], jnp.exp2(s_a).astype(BF16))
                      + _dot(vt_ref[2 * jj + 1], jnp.exp2(s_b).astype(BF16)))
                if track_max:
                    acc[...] = acc[...] * alpha + pv
                    ms[c][...] = m_new
                else:
                    acc[...] += pv
            return carry
        return body

    def all_tiles(track_max):
        qt0, qt1 = transposed_queries()
        corr = diagonal_correction()
        for a in range(0, nsub, 2):
            key_tiles(((nsub * qi + a, a * g, corr), (nsub * qi + a + 1, (a + 1) * g, corr)),
                      a == 0, track_max, qt0, qt1)
        lax.fori_loop(0, qi * (nsub // 2), make_body(track_max, qt0, qt1), 0)

    bound_is_safe = scal_ref[2 + DIFF_HEADS] > 0.5

    @pl.when(bound_is_safe)
    def _():
        all_tiles(False)

    @pl.when(jnp.logical_not(bound_is_safe))
    def _():
        all_tiles(True)

    a0 = acc0_ref[...]
    a1 = acc1_ref[...]
    ot = (a0[:DIFF_DV, :] / a0[DIFF_DV:DIFF_DV + 1, :]
          - lam * (a1[:DIFF_DV, :] / a1[DIFF_DV:DIFF_DV + 1, :]))
    mean_sq = jnp.mean(ot * ot, axis=0, keepdims=True)
    yt = (ot * lax.rsqrt(mean_sq + EPS) * subg_ref[...]) * out_scale
    o_ref[...] = yt.T.astype(BF16)


def _diff_attn(scal, qa, ka, vt, subg, *, batch, seq, tq):
    nq = seq // tq
    g = vt.shape[-1]
    assert tq % (2 * g) == 0
    return pl.pallas_call(
        _diff_attn_kernel,
        grid=(batch, DIFF_HEADS, nq),
        in_specs=[
            pl.BlockSpec(memory_space=pltpu.SMEM),
            pl.BlockSpec((tq, HEAD_W), lambda b, h, i: (b * nq + i, h)),
            pl.BlockSpec((seq, HEAD_W), lambda b, h, i: (b, h)),
            pl.BlockSpec((None, None, seq // g, V_ROWS, g), lambda b, h, i: (b, h, 0, 0, 0)),
            pl.BlockSpec((DIFF_DV, 1), lambda b, h, i: (0, 0)),
        ],
        out_specs=pl.BlockSpec((tq, DIFF_DV), lambda b, h, i: (b * nq + i, h)),
        out_shape=jax.ShapeDtypeStruct((batch * seq, DIFF_W), BF16),
        scratch_shapes=([pltpu.VMEM((V_ROWS, tq), F32)] * 2 + [pltpu.VMEM((1, tq), F32)] * 2
                        + [pltpu.VMEM((g, tq), F32)] * 4),
        compiler_params=pltpu.CompilerParams(
            dimension_semantics=("arbitrary", "arbitrary", "arbitrary"),
            vmem_limit_bytes=VMEM_LIMIT_BYTES),
        name="diff_attn",
    )(scal, qa, ka, vt, subg)


def _gla_kernel(zg_ref, za_ref, aw_ref, ab_ref, ng_ref, lmat_ref, amask_ref, bd_ref,
                o_ref, st_ref):
    @pl.when(pl.program_id(1) == 0)
    def _():
        st_ref[...] = jnp.zeros_like(st_ref)

    nb, r = zg_ref.shape[0], zg_ref.shape[1]
    seqs = range(nb)
    lane = lax.broadcasted_iota(jnp.int32, (1, GLA_W), 1)
    rowid = lax.broadcasted_iota(jnp.int32, (r, 1), 0)
    bd = bd_ref[...]
    bd_f = bd.astype(F32)

    log_a = []
    for bb in seqs:
        pre = _dot(za_ref[bb], aw_ref[...]) + ab_ref[...]
        log_a.append((jnp.minimum(pre, 0.0) - jnp.log(1.0 + jnp.exp(-jnp.abs(pre)))) * (1.0 / GLA_TAU))
    q_in, k_in, k_end, dec = [], [], [], []
    for bb in seqs:
        hi = log_a[bb].astype(BF16)
        lo = (log_a[bb] - hi.astype(F32)).astype(BF16)
        cs = _dot(lmat_ref[...], jnp.concatenate([hi, lo], axis=1))
        b = cs[:, :GLA_W] + cs[:, GLA_W:]
        b_last = [b[(c + 1) * CHUNK - 1:(c + 1) * CHUNK, :] for c in range(r // CHUNK)]
        b_tot = jnp.concatenate([jnp.broadcast_to(bl, (CHUNK, GLA_W)) for bl in b_last], axis=0)
        gq = zg_ref[bb, :, 0:GLA_W].astype(F32)
        gk = zg_ref[bb, :, GLA_W:2 * GLA_W].astype(F32)
        q_in.append(((gq * (GLA_DK ** -0.5)) * jnp.exp(b)).astype(BF16))
        k_in.append((gk * jnp.exp(-b)).astype(BF16))
        k_end.append(gk * jnp.exp(b_tot - b))
        dec.append([jnp.exp(bl) for bl in b_last])

    o = [jnp.zeros((r, GLA_W), F32) for _ in seqs]
    for hd in range(GLA_HEADS):
        in_head = (lane >= hd * GLA_DK) & (lane < (hd + 1) * GLA_DK)
        for bb in seqs:
            att = _nt_dot(jnp.where(in_head, q_in[bb], jnp.zeros_like(q_in[bb])), k_in[bb])
            att = jnp.where(amask_ref[...] > 0.0, att, 0.0).astype(BF16)
            o[bb] = jnp.where(in_head, _dot(att, zg_ref[bb, :, 2 * GLA_W:3 * GLA_W]), o[bb])

    kv_t = []
    for bb in seqs:
        gv_t = zg_ref[bb, :, 2 * GLA_W:3 * GLA_W].astype(F32).T.astype(BF16)
        per_chunk = []
        for c in range(r // CHUNK):
            in_chunk = (rowid >= c * CHUNK) & (rowid < (c + 1) * CHUNK)
            per_chunk.append(_dot(gv_t, jnp.where(in_chunk, k_end[bb], 0.0).astype(BF16)) * bd_f)
        kv_t.append(per_chunk)
    st = [st_ref[bb] for bb in seqs]
    inter = [[] for _ in seqs]
    for c in range(r // CHUNK):
        rows = slice(c * CHUNK, (c + 1) * CHUNK)
        for bb in seqs:
            inter[bb].append(_nt_dot(q_in[bb][rows, :], st[bb].astype(BF16)))
            st[bb] = st[bb] * dec[bb][c] + kv_t[bb][c]
    for bb in seqs:
        st_ref[bb] = st[bb]

    for bb in seqs:
        ob = o[bb] + jnp.concatenate(inter[bb], axis=0)
        ssq = _dot((ob * ob).astype(BF16), bd)
        on = ob * lax.rsqrt(ssq * (1.0 / GLA_DV) + EPS) * ng_ref[...]
        gg = zg_ref[bb, :, 3 * GLA_W:4 * GLA_W].astype(F32)
        o_ref[bb] = (on * (gg * (1.0 / (1.0 + jnp.exp(-gg))))).astype(BF16)


def _gla(zg, za, aw, ab, ng, lmat, amask, bd, *, batch, seq, r):
    nb = min(GLA_SEQS_PER_STEP, batch)
    row = lambda bg, t: (bg, t, 0)
    fixed = lambda bg, t: (0, 0)
    return pl.pallas_call(
        _gla_kernel,
        grid=(batch // nb, seq // r),
        in_specs=[
            pl.BlockSpec((nb, r, 4 * GLA_W), row),
            pl.BlockSpec((nb, r, LANES), row),
            pl.BlockSpec((LANES, GLA_W), fixed),
            pl.BlockSpec((1, GLA_W), fixed),
            pl.BlockSpec((1, GLA_W), fixed),
            pl.BlockSpec((r, r), fixed),
            pl.BlockSpec((r, r), fixed),
            pl.BlockSpec((GLA_W, GLA_W), fixed),
        ],
        out_specs=pl.BlockSpec((nb, r, GLA_W), row),
        out_shape=jax.ShapeDtypeStruct((batch, seq, GLA_W), BF16),
        scratch_shapes=[pltpu.VMEM((nb, GLA_W, GLA_W), F32)],
        compiler_params=pltpu.CompilerParams(
            dimension_semantics=("arbitrary", "arbitrary"), vmem_limit_bytes=VMEM_LIMIT_BYTES),
        name="gla",
    )(zg, za, aw, ab, ng, lmat, amask, bd)


def _out_mlp_kernel(x_ref, yc_ref, yd_ref, yg_ref, wo_ref, g2_ref, w1_ref, w2_ref,
                    o_ref, h2_ref):
    f = pl.program_id(1)

    @pl.when(f == 0)
    def _():
        sub = x_ref.shape[0] // MLP_SUBTILES
        for p in range(MLP_SUBTILES):
            rows = slice(p * sub, (p + 1) * sub)
            y = (_dot(yc_ref[rows, :], wo_ref[0:CONV_W, :])
                 + _dot(yd_ref[rows, :], wo_ref[CONV_W:CONV_W + DIFF_W, :])
                 + _dot(yg_ref[rows, :], wo_ref[CONV_W + DIFF_W:, :]))
            x1 = x_ref[rows, :] + y
            ms = jnp.mean(x1 * x1, axis=-1, keepdims=True)
            h2_ref[rows, :] = (x1 * lax.rsqrt(ms + EPS) * g2_ref[...]).astype(BF16)
            o_ref[rows, :] = x1

    a = jnp.maximum(_dot(h2_ref[...], w1_ref[...]), 0.0)
    o_ref[...] += _dot((a * a).astype(BF16), w2_ref[...])


def _out_mlp(x, yc, yd, yg, wo, g2, w1, w2, *, tm, tf, layer):
    t = x.shape[0]
    row = lambda i, f: (i, 0)
    fixed = lambda i, f: (0, 0)
    return pl.pallas_call(
        _out_mlp_kernel,
        grid=(t // tm, D_FF // tf),
        in_specs=[
            pl.BlockSpec((tm, D_MODEL), row),
            pl.BlockSpec((tm, CONV_W), row),
            pl.BlockSpec((tm, DIFF_W), row),
            pl.BlockSpec((tm, GLA_W), row),
            pl.BlockSpec((None, D_MODEL, D_MODEL), lambda i, f: (layer, 0, 0)),
            pl.BlockSpec((1, D_MODEL), fixed),
            pl.BlockSpec((None, D_MODEL, tf), lambda i, f: (layer, 0, f)),
            pl.BlockSpec((None, tf, D_MODEL), lambda i, f: (layer, f, 0)),
        ],
        out_specs=pl.BlockSpec((tm, D_MODEL), row),
        out_shape=jax.ShapeDtypeStruct((t, D_MODEL), F32),
        scratch_shapes=[pltpu.VMEM((tm, D_MODEL), BF16)],
        compiler_params=pltpu.CompilerParams(
            dimension_semantics=("arbitrary", "arbitrary"), vmem_limit_bytes=VMEM_LIMIT_BYTES),
        name="out_mlp",
    )(x, yc, yd, yg, wo, g2, w1, w2)


def _tile_sizes(seq):
    tk = min(ATTN_KEY_TILE, seq // 4)
    return dict(tk=tk, tm_in=2 * tk, tq=4 * tk, r_gla=min(GLA_ROWS, seq), tm_mlp=min(MLP_ROWS, seq),
                tf=MLP_HIDDEN_TILE)


def _block_diag_ones(n, blk):
    idx = jnp.arange(n) // blk
    return idx[:, None] == idx[None, :]


def _alibi_slopes2():
    return np.asarray([2.0 ** (-8.0 * (hd + 1) / DIFF_HEADS) * LOG2E for hd in range(DIFF_HEADS)],
                      np.float32)


def _position_columns(seq):
    width = DIFF_HEADS * AUG_W
    pos = lax.broadcasted_iota(jnp.int32, (seq, width), 0)
    col = lax.broadcasted_iota(jnp.int32, (seq, width), 1) % AUG_W
    c = jnp.asarray(np.repeat(_alibi_slopes2(), AUG_W))[None, :]
    c_hi = c.astype(BF16).astype(F32)
    c_lo = c - c_hi
    block = (pos - pos % LANES).astype(F32)
    rem = (pos % LANES).astype(F32)
    augq = jnp.where((col == 0) | (col == 2), c_hi,
                     jnp.where((col == 1) | (col == 3), c_lo,
                               jnp.where(col == 4, -c * pos.astype(F32), 0.0)))
    augk = jnp.where(col < 2, block, jnp.where(col < 4, rem, jnp.where(col == 4, 1.0, 0.0)))
    return augq.astype(BF16), augk.astype(BF16)


def kernel(x, ln1_g, w_in, conv_w, q_norm_g, k_norm_g, diff_lambda, diff_subln_g, gla_alpha_w,
           gla_alpha_b, gla_norm_g, w_out, ln2_g, w_mlp1, w_mlp2):
    batch, seq, d_model = x.shape
    depth = w_in.shape[0]
    assert d_model == D_MODEL and seq % (2 * CHUNK) == 0
    ts = _tile_sizes(seq)
    r = ts["r_gla"]

    bd_q = _block_diag_ones(DIFF_W, DIFF_DH).astype(BF16)
    bd_g = _block_diag_ones(GLA_W, GLA_DV).astype(BF16)
    same_chunk = _block_diag_ones(r, CHUNK)
    causal = jnp.arange(r)[:, None] >= jnp.arange(r)[None, :]
    lmat = (same_chunk & causal).astype(BF16)
    amask = (same_chunk & causal).astype(F32)
    augq, augk = _position_columns(seq)

    w_all = w_in.astype(BF16)
    w_a = jnp.pad(w_in[:, :, D_MAIN:], ((0, 0), (0, 0), (0, LANES - GLA_RANK))).astype(BF16)
    a_w = jnp.pad(gla_alpha_w, ((0, 0), (0, LANES - GLA_RANK), (0, 0))).astype(BF16)
    conv_w8 = jnp.pad(conv_w, ((0, 0), (0, SUBLANES - CONV_K), (0, 0)))
    w_o = w_out.astype(BF16)
    w_1 = w_mlp1.astype(BF16)
    w_2 = w_mlp2.astype(BF16)

    xf = x.reshape(batch * seq, D_MODEL)
    for l in range(depth):
        lam_init = 0.8 - 0.6 * math.exp(-0.3 * l)
        lp = diff_lambda[l].astype(F32)
        lam = jnp.exp(jnp.sum(lp[0] * lp[1])) - jnp.exp(jnp.sum(lp[2] * lp[3])) + lam_init
        bound2 = (1.02 * LOG2E * (DIFF_DH ** 0.5) * jnp.max(jnp.abs(q_norm_g[l])) * jnp.max(jnp.abs(k_norm_g[l]))
                  + float(_alibi_slopes2()[0]) * seq * 2.0 ** -9)
        bound_is_safe = (bound2 <= MAX_EXP2_ARG).astype(F32)
        scal = jnp.concatenate([jnp.stack([lam, jnp.asarray(1.0 - lam_init, F32)]),
                                jnp.asarray(_alibi_slopes2()), bound_is_safe[None]]).astype(F32)
        qg = jnp.tile(q_norm_g[l] * (DIFF_DH ** -0.5), 2 * DIFF_HEADS)[None, :]
        kg = jnp.tile(k_norm_g[l], 2 * DIFF_HEADS)[None, :]

        yconv, qa, ka, vt, zg, za = _in_proj(
            xf, ln1_g[l][None, :], w_all, w_a[l], conv_w8[l], qg, kg, bd_q,
            augq, augk, seq=seq, tm=ts["tm_in"], g=ts["tk"], layer=l)
        ydiff = _diff_attn(scal, qa, ka, vt, diff_subln_g[l][:, None],
                           batch=batch, seq=seq, tq=ts["tq"])
        ygla = _gla(zg.reshape(batch, seq, 4 * GLA_W), za.reshape(batch, seq, LANES), a_w[l],
                    gla_alpha_b[l][None, :], jnp.tile(gla_norm_g[l], GLA_HEADS)[None, :],
                    lmat, amask, bd_g, batch=batch, seq=seq, r=r)
        xf = _out_mlp(xf, yconv, ydiff, ygla.reshape(batch * seq, GLA_W), w_o, ln2_g[l][None, :],
                      w_1, w_2, tm=ts["tm_mlp"], tf=ts["tf"], layer=l)
    return xf.reshape(batch, seq, D_MODEL)
```
